```python
import math
import jax, jax.numpy as jnp
from jax import lax
import numpy as np

D_MODEL = 1024
BATCH = 2
SEQ = 8192
DEPTH = 4
DEC_BATCH = 128
DEC_SEQ = 8
PAST_LEN = 8192
PAGE_SIZE = 128

N_A_LAYERS = DEPTH // 2
N_B_LAYERS = DEPTH - N_A_LAYERS
D_INNER = 2 * D_MODEL
SSD_HEAD_DIM = 64
SSD_HEADS = D_INNER // SSD_HEAD_DIM
SSD_GROUPS = 4
D_STATE = 128
CONV_WIDTH = 4
CONV_DIM = D_INNER + 2 * SSD_GROUPS * D_STATE
SSD_CHUNK = 128
MLA_HEADS = 16
Q_LORA = D_MODEL // 2
KV_LORA = D_MODEL // 4
QK_NOPE = 64
QK_ROPE = 32
V_HEAD = 64
ROPE_BASE = 10000.0
MLA_SCALE = (QK_NOPE + QK_ROPE) ** -0.5
ATTN_BLOCK = 128
N_MEM = 256
MEM_HEADS = 4
MEM_HEAD_DIM = 256
MEM_WIDTH = MEM_HEADS * MEM_HEAD_DIM
D_FF = 4 * D_MODEL
RMS_EPS = 1e-6
IN_A = D_INNER + CONV_DIM + SSD_HEADS + MEM_WIDTH
IN_B = Q_LORA + MEM_WIDTH
OUT_A = D_INNER + MEM_WIDTH
OUT_B = MLA_HEADS * V_HEAD + MEM_WIDTH

kernel_name = 'yoco_ssd_mla_memory_decoder_step'

F32 = jnp.float32


def rms_norm(x, g):
    xf = x.astype(F32)
    y = xf * lax.rsqrt(jnp.mean(xf * xf, axis=-1, keepdims=True) + RMS_EPS)
    return (y * g.astype(F32)).astype(x.dtype)


def rope(x, pos):
    half = x.shape[-1] // 2
    inv = ROPE_BASE ** (-jnp.arange(half, dtype=F32) / half)
    ang = pos.astype(F32)[:, None] * inv[None, :]
    shape = (pos.shape[0],) + (1,) * (x.ndim - 3) + (half,)
    cos = jnp.cos(ang).reshape(shape)
    sin = jnp.sin(ang).reshape(shape)
    xf = x.astype(F32)
    x1, x2 = xf[..., :half], xf[..., half:]
    return jnp.concatenate([x1 * cos - x2 * sin, x1 * sin + x2 * cos], axis=-1).astype(x.dtype)


def causal_conv(u, prev, w, b):
    L = u.shape[1]
    up = jnp.concatenate([prev.astype(u.dtype), u], axis=1)
    out = b + sum(up[:, k:k + L] * w[k] for k in range(CONV_WIDTH))
    return out, up[:, up.shape[1] - (CONV_WIDTH - 1):]


def ssd_scan(xs, dt, A, Bm, Cm, h0):
    b, L, H, P = xs.shape
    G, N = Bm.shape[2], Bm.shape[3]
    R = H // G
    Q = SSD_CHUNK if L % SSD_CHUNK == 0 else L
    nc = L // Q
    a = (dt.astype(F32) * A.astype(F32)).reshape(b, nc, Q, G, R)
    acs = jnp.cumsum(a, axis=2)
    xdt = (xs.astype(F32) * dt.astype(F32)[..., None]).reshape(b, nc, Q, G, R, P)
    Bc = Bm.astype(F32).reshape(b, nc, Q, G, N)
    Cc = Cm.astype(F32).reshape(b, nc, Q, G, N)
    seg = acs[:, :, :, None] - acs[:, :, None, :]
    causal = jnp.tril(jnp.ones((Q, Q), dtype=bool))[:, :, None, None]
    decay = jnp.exp(jnp.where(causal, seg, -jnp.inf))
    cb = jnp.einsum('bclgn,bcsgn->bclsg', Cc, Bc)
    y_diag = jnp.einsum('bclsg,bclsgr,bcsgrp->bclgrp', cb, decay, xdt)
    a_last = acs[:, :, -1]
    w_state = jnp.exp(a_last[:, :, None] - acs)
    states = jnp.einsum('bclgn,bclgr,bclgrp->bcgrpn', Bc, w_state, xdt)

    def step(h, inp):
        st, al = inp
        return jnp.exp(al)[..., None, None] * h + st, h

    h0g = h0.astype(F32).reshape(b, G, R, P, N)
    h_final, h_enter = lax.scan(step, h0g, (jnp.moveaxis(states, 1, 0), jnp.moveaxis(a_last, 1, 0)))
    h_enter = jnp.moveaxis(h_enter, 0, 1)
    y_off = jnp.einsum('bclgn,bcgrpn,bclgr->bclgrp', Cc, h_enter, jnp.exp(acs))
    y = (y_diag + y_off).reshape(b, L, H, P)
    return y, h_final.reshape(b, H, P, N)


def mem_attend(q, mk, mv):
    s = jnp.einsum('bthd,bmhd->bhtm', q, mk).astype(F32) * (MEM_HEAD_DIM ** -0.5)
    p = jax.nn.softmax(s, axis=-1).astype(mv.dtype)
    o = jnp.einsum('bhtm,bmhd->bthd', p, mv)
    return o.reshape(q.shape[0], q.shape[1], MEM_WIDTH)


def mla_block_attend(q_lat, q_pe, ckv, kpe, q_pos, k_pos):
    s = (jnp.einsum('bthc,bsc->bhts', q_lat, ckv)
         + jnp.einsum('bthr,bsr->bhts', q_pe, kpe)).astype(F32) * MLA_SCALE
    s = jnp.where(k_pos[None, :] <= q_pos[:, None], s, -jnp.inf)
    p = jax.nn.softmax(s, axis=-1).astype(ckv.dtype)
    return jnp.einsum('bhts,bsc->bthc', p, ckv)


def mla_attend(q_lat, q_pe, ckv, kpe, q_pos, k_pos):
    b, T, H, C = q_lat.shape
    qb = ATTN_BLOCK if T % ATTN_BLOCK == 0 else T
    nb = T // qb
    ql = q_lat.reshape(b, nb, qb, H, C).swapaxes(0, 1)
    qp = q_pe.reshape(b, nb, qb, H, QK_ROPE).swapaxes(0, 1)
    pb = q_pos.reshape(nb, qb)
    o = lax.map(lambda a: mla_block_attend(a[0], a[1], ckv, kpe, a[2], k_pos), (ql, qp, pb))
    return o.swapaxes(0, 1).reshape(b, T, H, C)


def trunk(x, pos, past_ckv, past_kpe, mem_k, mem_v, ssm0, conv0, w):
    b, T, _ = x.shape
    ssm_out, conv_out = [], []
    new_ckv = new_kpe = k_ckv = k_kpe = k_pos = None
    for l in range(DEPTH):
        h = rms_norm(x, w['g_pre_mix'][l])
        if l < N_A_LAYERS:
            i = l
            proj = h @ w['w_in_a'][i]
            z = proj[..., :D_INNER]
            xbc = proj[..., D_INNER:D_INNER + CONV_DIM]
            dt_raw = proj[..., D_INNER + CONV_DIM:D_INNER + CONV_DIM + SSD_HEADS]
            mq = proj[..., IN_A - MEM_WIDTH:]
            xbc, conv_new = causal_conv(xbc, conv0[i], w['conv_w'][i], w['conv_b'][i])
            xbc = jax.nn.silu(xbc)
            xs = xbc[..., :D_INNER].reshape(b, T, SSD_HEADS, SSD_HEAD_DIM)
            Bm = xbc[..., D_INNER:D_INNER + SSD_GROUPS * D_STATE].reshape(b, T, SSD_GROUPS, D_STATE)
            Cm = xbc[..., D_INNER + SSD_GROUPS * D_STATE:].reshape(b, T, SSD_GROUPS, D_STATE)
            dt = jax.nn.softplus(dt_raw.astype(F32) + w['dt_bias'][i].astype(F32))
            A = -jnp.exp(w['a_log'][i].astype(F32))
            y, h_new = ssd_scan(xs, dt, A, Bm, Cm, ssm0[i])
            y = y + w['d_skip'][i].astype(F32)[:, None] * xs.astype(F32)
            y = y.reshape(b, T, D_INNER) * jax.nn.silu(z.astype(F32))
            y = rms_norm(y.reshape(b, T, SSD_GROUPS, D_INNER // SSD_GROUPS),
                         w['g_ssd_norm'][i].reshape(SSD_GROUPS, D_INNER // SSD_GROUPS))
            y = y.reshape(b, T, D_INNER).astype(x.dtype)
            mo = mem_attend(mq.reshape(b, T, MEM_HEADS, MEM_HEAD_DIM), mem_k[l], mem_v[l])
            mix = jnp.concatenate([y, mo], axis=-1) @ w['w_out_a'][i]
            ssm_out.append(h_new.astype(x.dtype))
            conv_out.append(conv_new)
        else:
            i = l - N_A_LAYERS
            if i == 0:
                kv = rms_norm(x, w['g_kv_in']) @ w['w_kv_a']
                new_ckv = rms_norm(kv[..., :KV_LORA], w['g_kv_norm'])
                new_kpe = rope(kv[..., KV_LORA:], pos)
                k_ckv = jnp.concatenate([past_ckv.astype(x.dtype), new_ckv], axis=1)
                k_kpe = jnp.concatenate([past_kpe.astype(x.dtype), new_kpe], axis=1)
                k_pos = jnp.arange(k_ckv.shape[1], dtype=jnp.int32)
            proj = h @ w['w_in_b'][i]
            qa = rms_norm(proj[..., :Q_LORA], w['g_q_norm'][i])
            mq = proj[..., Q_LORA:]
            q = jnp.einsum('btc,chd->bthd', qa, w['w_q_b'][i])
            q_lat = jnp.einsum('bthn,chn->bthc', q[..., :QK_NOPE], w['w_uk'])
            q_pe = rope(q[..., QK_NOPE:], pos)
            o_lat = mla_attend(q_lat, q_pe, k_ckv, k_kpe, pos, k_pos)
            o = jnp.einsum('bthc,chv->bthv', o_lat, w['w_uv']).reshape(b, T, MLA_HEADS * V_HEAD)
            mo = mem_attend(mq.reshape(b, T, MEM_HEADS, MEM_HEAD_DIM), mem_k[l], mem_v[l])
            mix = jnp.concatenate([o, mo], axis=-1) @ w['w_out_b'][i]
        x = x + rms_norm(mix, w['g_post_mix'][l])
        hf = rms_norm(x, w['g_pre_ffn'][l])
        f = jnp.square(jax.nn.relu(hf @ w['w_ffn_up'][l])) @ w['w_ffn_down'][l]
        x = x + rms_norm(f, w['g_post_ffn'][l])
    return x, jnp.stack(ssm_out), jnp.stack(conv_out), new_ckv, new_kpe


def setup_inputs(seed: int = 0) -> dict:
    key = jax.random.key(seed)
    ks = iter(jax.random.split(key, 64))

    def nrm(shape, scale):
        return jax.random.normal(next(ks), shape, F32) * scale

    def gain(shape):
        return 1.0 + nrm(shape, 0.05)

    n_pages = PAST_LEN // PAGE_SIZE
    n_used = DEC_BATCH * n_pages
    n_phys = n_used + max(1, n_used // 4)
    page_table = jax.random.permutation(next(ks), n_phys)[:n_used].reshape(DEC_BATCH, n_pages).astype(jnp.int32)

    dt0 = jnp.exp(jax.random.uniform(next(ks), (N_A_LAYERS, SSD_HEADS), F32, math.log(1e-3), math.log(1e-1)))
    dt_bias = dt0 + jnp.log(-jnp.expm1(-dt0))
    a_log = jnp.log(jax.random.uniform(next(ks), (N_A_LAYERS, SSD_HEADS), F32, 1.0, 16.0))

    return {
        'x_prompt': nrm((BATCH, SEQ, D_MODEL), 1.0),
        'x_sample': nrm((DEC_BATCH, DEC_SEQ, D_MODEL), 1.0),
        'mem_prompt': nrm((BATCH, N_MEM, D_MODEL), 1.0),
        'state_ssm': nrm((N_A_LAYERS, DEC_BATCH, SSD_HEADS, SSD_HEAD_DIM, D_STATE), 0.5),
        'state_conv': nrm((N_A_LAYERS, DEC_BATCH, CONV_WIDTH - 1, CONV_DIM), 1.0),
        'cache_ckv': nrm((n_phys, PAGE_SIZE, KV_LORA), 1.0),
        'cache_kpe': nrm((n_phys, PAGE_SIZE, QK_ROPE), 1.0),
        'cache_mem_k': nrm((DEPTH, DEC_BATCH, N_MEM, MEM_HEADS, MEM_HEAD_DIM), 1.0),
        'cache_mem_v': nrm((DEPTH, DEC_BATCH, N_MEM, MEM_HEADS, MEM_HEAD_DIM), 1.0),
        'page_table': page_table,
        'g_pre_mix': gain((DEPTH, D_MODEL)),
        'g_post_mix': gain((DEPTH, D_MODEL)),
        'g_pre_ffn': gain((DEPTH, D_MODEL)),
        'g_post_ffn': gain((DEPTH, D_MODEL)),
        'w_ffn_up': nrm((DEPTH, D_MODEL, D_FF), D_MODEL ** -0.5),
        'w_ffn_down': nrm((DEPTH, D_FF, D_MODEL), D_FF ** -0.5),
        'w_mem_k': nrm((DEPTH, D_MODEL, MEM_WIDTH), D_MODEL ** -0.5),
        'w_mem_v': nrm((DEPTH, D_MODEL, MEM_WIDTH), D_MODEL ** -0.5),
        'w_in_a': nrm((N_A_LAYERS, D_MODEL, IN_A), D_MODEL ** -0.5),
        'conv_w': nrm((N_A_LAYERS, CONV_WIDTH, CONV_DIM), CONV_WIDTH ** -0.5),
        'conv_b': nrm((N_A_LAYERS, CONV_DIM), 0.01),
        'dt_bias': dt_bias,
        'a_log': a_log,
        'd_skip': 1.0 + nrm((N_A_LAYERS, SSD_HEADS), 0.1),
        'g_ssd_norm': gain((N_A_LAYERS, D_INNER)),
        'w_out_a': nrm((N_A_LAYERS, OUT_A, D_MODEL), OUT_A ** -0.5),
        'g_kv_in': gain((D_MODEL,)),
        'w_kv_a': nrm((D_MODEL, KV_LORA + QK_ROPE), D_MODEL ** -0.5),
        'g_kv_norm': gain((KV_LORA,)),
        'w_uk': nrm((KV_LORA, MLA_HEADS, QK_NOPE), KV_LORA ** -0.5),
        'w_uv': nrm((KV_LORA, MLA_HEADS, V_HEAD), KV_LORA ** -0.5),
        'w_in_b': nrm((N_B_LAYERS, D_MODEL, IN_B), D_MODEL ** -0.5),
        'g_q_norm': gain((N_B_LAYERS, Q_LORA)),
        'w_q_b': nrm((N_B_LAYERS, Q_LORA, MLA_HEADS, QK_NOPE + QK_ROPE), Q_LORA ** -0.5),
        'w_out_b': nrm((N_B_LAYERS, OUT_B, D_MODEL), OUT_B ** -0.5),
    }


def reference(x_prompt, x_sample, mem_prompt, state_ssm, state_conv, cache_ckv, cache_kpe,
              cache_mem_k, cache_mem_v, page_table, g_pre_mix, g_post_mix, g_pre_ffn, g_post_ffn,
              w_ffn_up, w_ffn_down, w_mem_k, w_mem_v, w_in_a, conv_w, conv_b, dt_bias, a_log,
              d_skip, g_ssd_norm, w_out_a, g_kv_in, w_kv_a, g_kv_norm, w_uk, w_uv, w_in_b,
              g_q_norm, w_q_b, w_out_b):
    w = dict(g_pre_mix=g_pre_mix, g_post_mix=g_post_mix, g_pre_ffn=g_pre_ffn, g_post_ffn=g_post_ffn,
             w_ffn_up=w_ffn_up, w_ffn_down=w_ffn_down, w_in_a=w_in_a, conv_w=conv_w, conv_b=conv_b,
             dt_bias=dt_bias, a_log=a_log, d_skip=d_skip, g_ssd_norm=g_ssd_norm, w_out_a=w_out_a,
             g_kv_in=g_kv_in, w_kv_a=w_kv_a, g_kv_norm=g_kv_norm, w_uk=w_uk, w_uv=w_uv,
             w_in_b=w_in_b, g_q_norm=g_q_norm, w_q_b=w_q_b, w_out_b=w_out_b)
    dtype = x_prompt.dtype

    b, S, _ = x_prompt.shape
    p_mem_k_list = [(mem_prompt @ w_mem_k[l]).reshape(b, N_MEM, MEM_HEADS, MEM_HEAD_DIM) for l in range(DEPTH)]
    p_mem_v_list = [(mem_prompt @ w_mem_v[l]).reshape(b, N_MEM, MEM_HEADS, MEM_HEAD_DIM) for l in range(DEPTH)]
    pos_p = jnp.arange(S, dtype=jnp.int32)
    ssm0 = jnp.zeros((N_A_LAYERS, b, SSD_HEADS, SSD_HEAD_DIM, D_STATE), dtype)
    conv0 = jnp.zeros((N_A_LAYERS, b, CONV_WIDTH - 1, CONV_DIM), dtype)
    empty_ckv = jnp.zeros((b, 0, KV_LORA), dtype)
    empty_kpe = jnp.zeros((b, 0, QK_ROPE), dtype)
    y_prompt, p_ssm, p_conv, p_ckv, p_kpe = trunk(x_prompt, pos_p, empty_ckv, empty_kpe,
                                                  p_mem_k_list, p_mem_v_list, ssm0, conv0, w)
    p_mem_k = jnp.stack(p_mem_k_list)
    p_mem_v = jnp.stack(p_mem_v_list)

    db, T, _ = x_sample.shape
    past_len = page_table.shape[1] * cache_ckv.shape[1]
    past_ckv = cache_ckv[page_table].reshape(db, past_len, KV_LORA)
    past_kpe = cache_kpe[page_table].reshape(db, past_len, QK_ROPE)
    pos_s = past_len + jnp.arange(T, dtype=jnp.int32)
    s_mem_k_list = [cache_mem_k[l] for l in range(DEPTH)]
    s_mem_v_list = [cache_mem_v[l] for l in range(DEPTH)]
    y_sample, s_ssm, s_conv, s_ckv, s_kpe = trunk(x_sample, pos_s, past_ckv, past_kpe,
                                                  s_mem_k_list, s_mem_v_list, state_ssm, state_conv, w)

    return (y_prompt, y_sample, p_ssm, p_conv, p_ckv, p_kpe, p_mem_k, p_mem_v, s_ssm, s_conv, s_ckv, s_kpe)
```

```python
import functools

import jax
import jax.numpy as jnp
from jax import lax
from jax.experimental import pallas as pl
from jax.experimental.pallas import tpu as pltpu

F32 = jnp.float32
BF16 = jnp.bfloat16

D_MODEL = 1024
DEPTH = 4
N_A = 2
D_INNER = 2048
SSD_HEADS = 32
SSD_HEAD_DIM = 64
SSD_GROUPS = 4
GROUP_W = D_INNER // SSD_GROUPS
D_STATE = 128
CONV_DIM = 3072
SSD_CHUNK = 128
MLA_HEADS = 16
Q_LORA = 512
KV_LORA = 256
QK_NOPE = 64
QK_ROPE = 32
V_HEAD = 64
ROPE_BASE = 10000.0
MLA_SCALE = (QK_NOPE + QK_ROPE) ** -0.5
N_MEM = 256
MEM_HEADS = 4
MEM_HEAD_DIM = 256
MEM_WIDTH = 1024
D_FF = 4096
RMS_EPS = 1e-6
LANES = 128
LAT_W = 384
NEG = -1e30
ROW_TILE = 512
ATT_TILE = 1024
ATT_SUB = 512
PAGES_PER_STEP = 16

NT_DIMS = (((1,), (1,)), ((), ()))
TN_DIMS = (((0,), (0,)), ((), ()))


def _cparams(sem, vmem_mb=48):
    return pltpu.CompilerParams(dimension_semantics=sem, vmem_limit_bytes=vmem_mb * 1024 * 1024)


def _rms(x, g):
    ms = jnp.mean(x * x, axis=-1, keepdims=True)
    return x * lax.rsqrt(ms + RMS_EPS) * g


def _silu(x):
    return x * (1.0 / (1.0 + jnp.exp(-x)))


def _softplus(x):
    return jnp.maximum(x, 0.0) + jnp.log1p(jnp.exp(-jnp.abs(x)))


def _dot(a, b):
    return jnp.dot(a, b, preferred_element_type=F32)


def _dot_nt(a, b):
    return lax.dot_general(a, b, NT_DIMS, preferred_element_type=F32)


def _dot_tn(a, b):
    return lax.dot_general(a, b, TN_DIMS, preferred_element_type=F32)


def _split_bf16(v, terms):
    parts = []
    for _ in range(terms):
        p = v.astype(BF16)
        parts.append(p)
        v = v - p.astype(F32)
    return parts


def _sel_dot(m, v, terms=2):
    return sum(_dot(m, p) for p in _split_bf16(v, terms))


def _expand(v, e_ref, terms=2):
    return sum(_dot(p, e_ref[...]) for p in _split_bf16(v, terms))


def _proj_kernel(*refs, has_g, has_rope, scale):
    it = iter(refs)
    x_ref = next(it)
    g_ref = next(it) if has_g else None
    w_ref = next(it)
    if has_rope:
        w2_ref, cos_ref, sin_ref = next(it), next(it), next(it)
    o_ref = next(it)
    xn_ref = next(it)

    @pl.when(pl.program_id(1) == 0)
    def _():
        x = x_ref[...].astype(F32)
        if has_g:
            x = _rms(x, g_ref[...])
        xn_ref[...] = x.astype(BF16)

    xn = xn_ref[...]
    y = _dot(xn, w_ref[...])
    if has_rope:
        rep = y.shape[1] // LANES
        cos = jnp.concatenate([cos_ref[...]] * rep, axis=1)
        sin = jnp.concatenate([sin_ref[...]] * rep, axis=1)
        y = (y * cos + _dot(xn, w2_ref[...]) * sin) * scale
    o_ref[...] = y.astype(o_ref.dtype)


def _proj(x, w, *, g=None, xcol=0, out_dtype=BF16, tn=None, rope=None, name):
    n = x.shape[0]
    k, nout = w.shape
    tm = min(ROW_TILE, n)
    tn = nout if tn is None else tn
    in_specs = [pl.BlockSpec((tm, k), lambda i, j: (i, xcol))]
    args = [x]
    if g is not None:
        in_specs.append(pl.BlockSpec((1, k), lambda i, j: (0, 0)))
        args.append(g.reshape(1, k).astype(F32))
    in_specs.append(pl.BlockSpec((k, tn), lambda i, j: (0, j)))
    args.append(w)
    scale = 1.0
    if rope is not None:
        w2, cos, sin, scale = rope
        nb = cos.shape[0] // tm
        in_specs.append(pl.BlockSpec((k, tn), lambda i, j: (0, j)))
        in_specs.append(pl.BlockSpec((tm, LANES), lambda i, j: (i % nb, 0)))
        in_specs.append(pl.BlockSpec((tm, LANES), lambda i, j: (i % nb, 0)))
        args += [w2, cos, sin]
    return pl.pallas_call(
        functools.partial(_proj_kernel, has_g=g is not None, has_rope=rope is not None, scale=scale),
        out_shape=jax.ShapeDtypeStruct((n, nout), out_dtype),
        grid=(n // tm, nout // tn),
        in_specs=in_specs,
        out_specs=pl.BlockSpec((tm, tn), lambda i, j: (i, j)),
        scratch_shapes=[pltpu.VMEM((tm, k), BF16)],
        compiler_params=_cparams(("parallel", "arbitrary")),
        name=name,
    )(*args)


def _kvlat_kernel(x_ref, g_ref, wa_ref, wr_ref, gkv_ref, cos_ref, sin_ref, ckv_ref, kpe_ref, lat_ref):
    xn = _rms(x_ref[...], g_ref[...]).astype(BF16)
    a = _dot(xn, wa_ref[...])
    r = _dot(xn, wr_ref[...])
    ckv = _rms(a[:, :KV_LORA], gkv_ref[...])
    pe = a[:, KV_LORA:] * cos_ref[...] + r * sin_ref[...]
    ckv_ref[...] = ckv
    kpe_ref[...] = pe
    lat_ref[:, :KV_LORA] = ckv.astype(BF16)
    lat_ref[:, KV_LORA:] = pe.astype(BF16)


def _kv_latent(x, g, wa, wr, gkv, cos, sin):
    n = x.shape[0]
    tm = min(ROW_TILE, n)
    nb = cos.shape[0] // tm
    row = lambda i: (i, 0)
    const = lambda i: (0, 0)
    return pl.pallas_call(
        _kvlat_kernel,
        out_shape=(jax.ShapeDtypeStruct((n, KV_LORA), F32),
                   jax.ShapeDtypeStruct((n, LANES), F32),
                   jax.ShapeDtypeStruct((n, LAT_W), BF16)),
        grid=(n // tm,),
        in_specs=[pl.BlockSpec((tm, D_MODEL), row),
                  pl.BlockSpec((1, D_MODEL), const),
                  pl.BlockSpec((D_MODEL, LAT_W), const),
                  pl.BlockSpec((D_MODEL, LANES), const),
                  pl.BlockSpec((1, KV_LORA), const),
                  pl.BlockSpec((tm, LANES), lambda i: (i % nb, 0)),
                  pl.BlockSpec((tm, LANES), lambda i: (i % nb, 0))],
        out_specs=(pl.BlockSpec((tm, KV_LORA), row),
                   pl.BlockSpec((tm, LANES), row),
                   pl.BlockSpec((tm, LAT_W), row)),
        compiler_params=_cparams(("parallel",)),
        name="kv_latent",
    )(x, g.reshape(1, D_MODEL), wa, wr, gkv.reshape(1, KV_LORA), cos, sin)


def _outproj_kernel(a1_ref, a2_ref, w1_ref, w2_ref, g_ref, r_ref, o_ref):
    acc = _dot(a1_ref[...].astype(BF16), w1_ref[...]) + _dot(a2_ref[...].astype(BF16), w2_ref[...])
    o_ref[...] = r_ref[...] + _rms(acc, g_ref[...])


def _outproj(a1, a2, w1, w2, g, res, name):
    n = res.shape[0]
    tm = min(ROW_TILE, n)
    k1, k2 = w1.shape[0], w2.shape[0]
    row = lambda i: (i, 0)
    const = lambda i: (0, 0)
    return pl.pallas_call(
        _outproj_kernel,
        out_shape=jax.ShapeDtypeStruct((n, D_MODEL), F32),
        grid=(n // tm,),
        in_specs=[pl.BlockSpec((tm, k1), row), pl.BlockSpec((tm, k2), row),
                  pl.BlockSpec((k1, D_MODEL), const), pl.BlockSpec((k2, D_MODEL), const),
                  pl.BlockSpec((1, D_MODEL), const), pl.BlockSpec((tm, D_MODEL), row)],
        out_specs=pl.BlockSpec((tm, D_MODEL), row),
        compiler_params=_cparams(("parallel",)),
        name=name,
    )(a1, a2, w1, w2, g.reshape(1, D_MODEL), res)


def _ffn_kernel(x_ref, g1_ref, wu_ref, wd_ref, g2_ref, o_ref):
    x = x_ref[...]
    xn = _rms(x, g1_ref[...]).astype(BF16)
    acc = jnp.zeros(x.shape, F32)
    for c in range(D_FF // D_MODEL):
        sl = slice(c * D_MODEL, (c + 1) * D_MODEL)
        h = _dot(xn, wu_ref[:, sl])
        h = jnp.square(jnp.maximum(h, 0.0)).astype(BF16)
        acc = acc + _dot(h, wd_ref[sl, :])
    o_ref[...] = x + _rms(acc, g2_ref[...])


def _ffn(x, g1, wu, wd, g2, name):
    n = x.shape[0]
    tm = min(ROW_TILE, n)
    row = lambda i: (i, 0)
    const = lambda i: (0, 0)
    return pl.pallas_call(
        _ffn_kernel,
        out_shape=jax.ShapeDtypeStruct((n, D_MODEL), F32),
        grid=(n // tm,),
        in_specs=[pl.BlockSpec((tm, D_MODEL), row), pl.BlockSpec((1, D_MODEL), const),
                  pl.BlockSpec((D_MODEL, D_FF), const, pipeline_mode=pl.Buffered(1)),
                  pl.BlockSpec((D_FF, D_MODEL), const, pipeline_mode=pl.Buffered(1)),
                  pl.BlockSpec((1, D_MODEL), const)],
        out_specs=pl.BlockSpec((tm, D_MODEL), row),
        compiler_params=_cparams(("parallel",), vmem_mb=48),
        name=name,
    )(x, g1.reshape(1, D_MODEL), wu, wd, g2.reshape(1, D_MODEL))


def _mem_kernel(q_ref, k_ref, v_ref, o_ref):
    q = q_ref[...]
    scale = MEM_HEAD_DIM ** -0.5
    for h in range(MEM_HEADS):
        sl = slice(h * MEM_HEAD_DIM, (h + 1) * MEM_HEAD_DIM)
        s = _dot_nt(q[:, sl].astype(BF16), k_ref[0, :, sl].astype(BF16)) * scale
        m = jnp.max(s, axis=-1, keepdims=True)
        p = jnp.exp(s - m)
        l = jnp.sum(p, axis=-1, keepdims=True)
        o = _dot(p.astype(BF16), v_ref[0, :, sl].astype(BF16)) / l
        o_ref[:, sl] = o.astype(o_ref.dtype)


def _mem_attend(q, qcol, k_arr, v_arr, kboff, kcol, vcol, batch, t, out_dtype, name):
    tq = min(ROW_TILE, t)
    nt = t // tq
    return pl.pallas_call(
        _mem_kernel,
        out_shape=jax.ShapeDtypeStruct((batch * t, MEM_WIDTH), out_dtype),
        grid=(batch, nt),
        in_specs=[pl.BlockSpec((tq, MEM_WIDTH), lambda b, i: (b * nt + i, qcol)),
                  pl.BlockSpec((1, N_MEM, MEM_WIDTH), lambda b, i: (b + kboff, 0, kcol)),
                  pl.BlockSpec((1, N_MEM, MEM_WIDTH), lambda b, i: (b + kboff, 0, vcol))],
        out_specs=pl.BlockSpec((tq, MEM_WIDTH), lambda b, i: (b * nt + i, 0)),
        compiler_params=_cparams(("parallel", "parallel")),
        name=name,
    )(q, k_arr, v_arr)


def _gate_norm_store(y_ref, ygrp, z_ref, gn_ref, g):
    sl = slice(g * GROUP_W, (g + 1) * GROUP_W)
    ygrp = ygrp * _silu(z_ref[:, sl].astype(F32))
    y_ref[:, sl] = _rms(ygrp, gn_ref[:, sl]).astype(y_ref.dtype)


def _ssd_prompt_kernel(xs_ref, bc_ref, z_ref, dt_ref, cw_ref, cb_ref, dtb_ref, alog_ref, d_ref, gn_ref, e_ref,
                       y_ref, hout_ref, extx, extbc, ht):
    q = SSD_CHUNK
    c = pl.program_id(1)

    @pl.when(c == 0)
    def _():
        extx[0:8] = jnp.zeros((8, D_INNER), F32)
        extbc[0:8] = jnp.zeros((8, CONV_DIM - D_INNER), F32)
        ht[...] = jnp.zeros(ht.shape, F32)

    @pl.when(c > 0)
    def _():
        extx[0:8] = extx[q:q + 8]
        extbc[0:8] = extbc[q:q + 8]

    extx[8:q + 8] = xs_ref[...].astype(F32)
    extbc[8:q + 8] = bc_ref[...].astype(F32)

    def conv(ext, w, b):
        acc = b + w[0:1] * ext[5:q + 5] + w[1:2] * ext[6:q + 6] + w[2:3] * ext[7:q + 7] + w[3:4] * ext[8:q + 8]
        return _silu(acc)

    cw = cw_ref[...]
    cb = cb_ref[...]
    xs = conv(extx, cw[:, :D_INNER], cb[:, :D_INNER])
    bcv = conv(extbc, cw[:, D_INNER:], cb[:, D_INNER:])

    dt = _softplus(dt_ref[...] + dtb_ref[...])
    a = dt * (-jnp.exp(alog_ref[...]))
    row = lax.broadcasted_iota(jnp.int32, (q, q), 0)
    col = lax.broadcasted_iota(jnp.int32, (q, q), 1)
    causal = row >= col
    acs = _sel_dot(jnp.where(causal, 1.0, 0.0).astype(BF16), a, terms=3)
    acs_t = acs.T
    a_last = acs[q - 1:q, :]
    dtx = _expand(dt, e_ref)
    dwx = _expand(dt * jnp.exp(a_last - acs), e_ref)
    eax = _expand(jnp.exp(acs), e_ref)
    xdt = xs * dtx
    xw_b = (xs * dwx).astype(BF16)
    lane = lax.broadcasted_iota(jnp.int32, (q, LANES), 1)
    heads_per_group = SSD_HEADS // SSD_GROUPS

    for g in range(SSD_GROUPS):
        gs = slice(g * GROUP_W, (g + 1) * GROUP_W)
        bg = bcv[:, g * D_STATE:(g + 1) * D_STATE].astype(BF16)
        cg = bcv[:, (SSD_GROUPS + g) * D_STATE:(SSD_GROUPS + g + 1) * D_STATE].astype(BF16)
        cbm = _dot_nt(cg, bg)
        htg = ht[:, gs]
        y_off = _dot(cg, htg.astype(BF16)) * eax[:, gs]
        ht[:, gs] = eax[q - 1:q, gs] * htg + _dot_tn(bg, xw_b[:, gs])
        pairs = []
        for j in range(heads_per_group // 2):
            h0 = g * heads_per_group + 2 * j
            xpair = xdt[:, h0 * SSD_HEAD_DIM:(h0 + 2) * SSD_HEAD_DIM]
            yp = None
            for k in range(2):
                h = h0 + k
                seg = acs[:, h:h + 1] - acs_t[h:h + 1, :]
                m = (cbm * jnp.exp(jnp.where(causal, seg, NEG))).astype(BF16)
                keep = (lane < SSD_HEAD_DIM) if k == 0 else (lane >= SSD_HEAD_DIM)
                t = _dot(m, jnp.where(keep, xpair, 0.0).astype(BF16))
                yp = t if yp is None else yp + t
            pairs.append(yp)
        ygrp = jnp.concatenate(pairs, axis=1) + y_off + d_ref[:, gs] * xs[:, gs]
        _gate_norm_store(y_ref, ygrp, z_ref, gn_ref, g)

    @pl.when(c == pl.num_programs(1) - 1)
    def _():
        hout_ref[0] = ht[...].T


def _ssd_prompt(proj, dt, sp, batch, t):
    nc = t // SSD_CHUNK
    q = SSD_CHUNK
    const = lambda b, c: (0, 0)
    return pl.pallas_call(
        _ssd_prompt_kernel,
        out_shape=(jax.ShapeDtypeStruct((batch * t, D_INNER), BF16),
                   jax.ShapeDtypeStruct((batch, D_INNER, D_STATE), F32)),
        grid=(batch, nc),
        in_specs=[pl.BlockSpec((q, D_INNER), lambda b, c: (b * nc + c, 0)),
                  pl.BlockSpec((q, 1024), lambda b, c: (b * nc + c, 2)),
                  pl.BlockSpec((q, D_INNER), lambda b, c: (b * nc + c, 2)),
                  pl.BlockSpec((q, LANES), lambda b, c: (b * nc + c, 0)),
                  pl.BlockSpec((4, CONV_DIM), const), pl.BlockSpec((1, CONV_DIM), const),
                  pl.BlockSpec((1, LANES), const), pl.BlockSpec((1, LANES), const),
                  pl.BlockSpec((1, D_INNER), const), pl.BlockSpec((1, D_INNER), const),
                  pl.BlockSpec((LANES, D_INNER), const)],
        out_specs=(pl.BlockSpec((q, D_INNER), lambda b, c: (b * nc + c, 0)),
                   pl.BlockSpec((1, D_INNER, D_STATE), lambda b, c: (b, 0, 0))),
        scratch_shapes=[pltpu.VMEM((q + 8, D_INNER), F32),
                        pltpu.VMEM((q + 8, CONV_DIM - D_INNER), F32),
                        pltpu.VMEM((D_STATE, D_INNER), F32)],
        compiler_params=_cparams(("parallel", "arbitrary"), vmem_mb=48),
        name="ssd_prompt",
    )(proj, proj, proj, dt, sp["conv_w"], sp["conv_b"], sp["dt_bias"], sp["a_log"], sp["d_x"], sp["g_norm"], sp["expand"])


def _ssd_sample_kernel(xs_ref, bc_ref, z_ref, dt_ref, cs_ref, h0_ref, cw_ref, cb_ref, dtb_ref, alog_ref, d_ref,
                       gn_ref, e_ref, y_ref, cnew_ref, hnew_ref, ext):
    t = xs_ref.shape[0]
    ext[5:8] = cs_ref[0]
    ext[8:8 + t, 0:D_INNER] = xs_ref[...]
    ext[8:8 + t, D_INNER:CONV_DIM] = bc_ref[...]
    cw = cw_ref[...]
    acc = (cb_ref[...] + cw[0:1] * ext[5:5 + t] + cw[1:2] * ext[6:6 + t]
           + cw[2:3] * ext[7:7 + t] + cw[3:4] * ext[8:8 + t])
    cnew_ref[0] = ext[5 + t:8 + t]
    xbc = _silu(acc)
    xs = xbc[:, :D_INNER]

    dt = _softplus(dt_ref[...] + dtb_ref[...])
    a = dt * (-jnp.exp(alog_ref[...]))
    rowi = lax.broadcasted_iota(jnp.int32, (t, LANES), 0)
    acs = jnp.zeros((t, LANES), F32)
    for s in range(t):
        acs = acs + jnp.where(rowi >= s, a[s:s + 1, :], 0.0)
    a_last = acs[t - 1:t, :]
    eacs = jnp.exp(acs)
    acsx = _expand(acs, e_ref, terms=3)
    dtx = _expand(dt, e_ref)
    dwx = _expand(dt * jnp.exp(a_last - acs), e_ref)
    eax = _expand(eacs, e_ref)
    xdt = xs * dtx
    xw_b = (xs * dwx).astype(BF16)

    def grp(base, g):
        return xbc[:, D_INNER + (base + g) * D_STATE:D_INNER + (base + g + 1) * D_STATE]

    li = lax.broadcasted_iota(jnp.int32, (t, D_INNER), 0)
    y = jnp.zeros((t, D_INNER), F32)
    for s in range(t):
        decay = jnp.exp(jnp.where(li >= s, acsx - acsx[s:s + 1, :], NEG))
        cbx = jnp.concatenate(
            [jnp.broadcast_to(jnp.sum(grp(SSD_GROUPS, g) * grp(0, g)[s:s + 1, :], axis=-1, keepdims=True),
                              (t, GROUP_W)) for g in range(SSD_GROUPS)], axis=1)
        y = y + cbx * decay * xdt[s:s + 1, :]

    last_only = jnp.where(rowi == t - 1, eacs, 0.0)
    ones = jnp.ones((t, LANES), BF16)
    rdec = sum(_dot_tn(p, ones) for p in _split_bf16(last_only, 3))

    heads_per_group = SSD_HEADS // SSD_GROUPS
    for g in range(SSD_GROUPS):
        gs = slice(g * GROUP_W, (g + 1) * GROUP_W)
        bg = grp(0, g).astype(BF16)
        cg = grp(SSD_GROUPS, g).astype(BF16)
        h0g = h0_ref[0, gs, :]
        y_off = _dot_nt(cg, h0g.astype(BF16)) * eax[:, gs]
        upd = _dot_tn(xw_b[:, gs], bg)
        for hh in range(heads_per_group):
            h = g * heads_per_group + hh
            rs = slice(hh * SSD_HEAD_DIM, (hh + 1) * SSD_HEAD_DIM)
            hnew_ref[0, h * SSD_HEAD_DIM:(h + 1) * SSD_HEAD_DIM, :] = rdec[h:h + 1, :] * h0g[rs, :] + upd[rs, :]
        ygrp = y[:, gs] + y_off + d_ref[:, gs] * xs[:, gs]
        _gate_norm_store(y_ref, ygrp, z_ref, gn_ref, g)


def _ssd_sample(proj, dt, conv_state, h0, sp, batch, t):
    const = lambda b: (0, 0)
    return pl.pallas_call(
        _ssd_sample_kernel,
        out_shape=(jax.ShapeDtypeStruct((batch * t, D_INNER), F32),
                   jax.ShapeDtypeStruct((batch, 3, CONV_DIM), F32),
                   jax.ShapeDtypeStruct((batch, D_INNER, D_STATE), F32)),
        grid=(batch,),
        in_specs=[pl.BlockSpec((t, D_INNER), lambda b: (b, 0)),
                  pl.BlockSpec((t, 1024), lambda b: (b, 2)),
                  pl.BlockSpec((t, D_INNER), lambda b: (b, 2)),
                  pl.BlockSpec((t, LANES), lambda b: (b, 0)),
                  pl.BlockSpec((1, 3, CONV_DIM), lambda b: (b, 0, 0)),
                  pl.BlockSpec((1, D_INNER, D_STATE), lambda b: (b, 0, 0)),
                  pl.BlockSpec((4, CONV_DIM), const), pl.BlockSpec((1, CONV_DIM), const),
                  pl.BlockSpec((1, LANES), const), pl.BlockSpec((1, LANES), const),
                  pl.BlockSpec((1, D_INNER), const), pl.BlockSpec((1, D_INNER), const),
                  pl.BlockSpec((LANES, D_INNER), const)],
        out_specs=(pl.BlockSpec((t, D_INNER), lambda b: (b, 0)),
                   pl.BlockSpec((1, 3, CONV_DIM), lambda b: (b, 0, 0)),
                   pl.BlockSpec((1, D_INNER, D_STATE), lambda b: (b, 0, 0))),
        scratch_shapes=[pltpu.VMEM((8 + t, CONV_DIM), F32)],
        compiler_params=_cparams(("parallel",)),
        name="ssd_sample",
    )(proj, proj, proj, dt, conv_state, h0, sp["conv_w"], sp["conv_b"], sp["dt_bias"], sp["a_log"], sp["d_x"],
      sp["g_norm"], sp["expand"])


def _flash_kernel(qt_ref, kt_ref, q_ref, k_ref, v_ref, o_ref, m_ref, l_ref, acc_ref, *, sub):
    step = pl.program_id(2)
    qi = qt_ref[step]
    ki = kt_ref[step]
    nsub = q_ref.shape[0] // sub
    first = lax.broadcasted_iota(jnp.int32, (sub, LANES), 1) < V_HEAD

    @pl.when(ki == 0)
    def _():
        m_ref[...] = jnp.full(m_ref.shape, NEG, F32)
        l_ref[...] = jnp.zeros(l_ref.shape, F32)
        acc_ref[...] = jnp.zeros(acc_ref.shape, F32)

    def unit(qs, ks, masked):
        rows = slice(qs * sub, (qs + 1) * sub)
        cols = slice(ks * sub, (ks + 1) * sub)
        vf = v_ref[cols, :].astype(F32)
        pv = None
        alphas = []
        for h in range(2):
            hs = slice(h * LANES, (h + 1) * LANES)
            s = _dot_nt(q_ref[rows, hs], k_ref[cols, hs])
            if masked:
                r = lax.broadcasted_iota(jnp.int32, (sub, sub), 0)
                c = lax.broadcasted_iota(jnp.int32, (sub, sub), 1)
                s = jnp.where(r >= c, s, NEG)
            m_prev = m_ref[h, rows, :]
            m_new = jnp.maximum(m_prev, jnp.max(s, axis=-1, keepdims=True))
            p = jnp.exp(s - jnp.concatenate([m_new] * (sub // LANES), axis=1))
            alpha = jnp.exp(m_prev - m_new)
            l_ref[h, rows, :] = alpha * l_ref[h, rows, :] + jnp.sum(p, axis=-1, keepdims=True)
            m_ref[h, rows, :] = m_new
            vh = jnp.where(first if h == 0 else jnp.logical_not(first), vf, 0.0).astype(BF16)
            t = _dot(p.astype(BF16), vh)
            pv = t if pv is None else pv + t
            alphas.append(alpha)
        acc_ref[rows, :] = acc_ref[rows, :] * jnp.where(first, alphas[0], alphas[1]) + pv

    @pl.when(ki < qi)
    def _():
        for qs in range(nsub):
            for ks in range(nsub):
                unit(qs, ks, False)

    @pl.when(ki == qi)
    def _():
        for qs in range(nsub):
            for ks in range(qs + 1):
                unit(qs, ks, ks == qs)
        first_all = lax.broadcasted_iota(jnp.int32, acc_ref.shape, 1) < V_HEAD
        o_ref[...] = (acc_ref[...] / jnp.where(first_all, l_ref[0], l_ref[1])).astype(o_ref.dtype)


def _flash_attend(q2, k_all, v_all, batch, t):
    tq = min(ATT_TILE, t)
    nq = t // tq
    pairs = [(qi, ki) for qi in range(nq) for ki in range(qi + 1)]
    qt = jnp.asarray([p[0] for p in pairs], jnp.int32)
    kt = jnp.asarray([p[1] for p in pairs], jnp.int32)
    grid_spec = pltpu.PrefetchScalarGridSpec(
        num_scalar_prefetch=2,
        grid=(batch, MLA_HEADS // 2, len(pairs)),
        in_specs=[pl.BlockSpec((tq, 2 * LANES), lambda b, hp, s, qt, kt: (b * nq + qt[s], hp)),
                  pl.BlockSpec((tq, 2 * LANES), lambda b, hp, s, qt, kt: (b * nq + kt[s], hp)),
                  pl.BlockSpec((tq, 2 * V_HEAD), lambda b, hp, s, qt, kt: (b * nq + kt[s], hp))],
        out_specs=pl.BlockSpec((tq, 2 * V_HEAD), lambda b, hp, s, qt, kt: (b * nq + qt[s], hp)),
        scratch_shapes=[pltpu.VMEM((2, tq, LANES), F32), pltpu.VMEM((2, tq, LANES), F32),
                        pltpu.VMEM((tq, LANES), F32)],
    )
    return pl.pallas_call(
        functools.partial(_flash_kernel, sub=min(ATT_SUB, tq)),
        out_shape=jax.ShapeDtypeStruct((batch * t, MLA_HEADS * V_HEAD), BF16),
        grid_spec=grid_spec,
        compiler_params=_cparams(("parallel", "parallel", "arbitrary")),
        name="mla_flash",
    )(qt, kt, q2, k_all, v_all)


def _decode_kernel(pt_ref, q_ref, cn_ref, pn_ref, *rest, g_pages):
    del pt_ref
    ckv_refs = rest[:g_pages]
    kpe_refs = rest[g_pages:2 * g_pages]
    o_ref, m_ref, l_ref, acc_ref = rest[2 * g_pages:]
    j = pl.program_id(1)
    rows = q_ref.shape[1]

    @pl.when(j == 0)
    def _():
        m_ref[...] = jnp.full(m_ref.shape, NEG, F32)
        l_ref[...] = jnp.zeros(l_ref.shape, F32)
        acc_ref[...] = jnp.zeros(acc_ref.shape, F32)

    q = q_ref[0]
    ql = q[:, :KV_LORA]
    qp = q[:, KV_LORA:KV_LORA + QK_ROPE]

    def update(s, values):
        m_prev = m_ref[...]
        m_new = jnp.maximum(m_prev, jnp.max(s, axis=-1, keepdims=True))
        p = jnp.exp(s - jnp.concatenate([m_new] * (s.shape[1] // LANES), axis=1))
        alpha = jnp.exp(m_prev - m_new)
        l_ref[...] = alpha * l_ref[...] + jnp.sum(p, axis=-1, keepdims=True)
        m_ref[...] = m_new
        pb = p.astype(BF16)
        pv = sum(_dot(pb[:, i * LANES:(i + 1) * LANES], v) for i, v in enumerate(values))
        acc_ref[...] = acc_ref[...] * jnp.concatenate([alpha] * (KV_LORA // LANES), axis=1) + pv

    kcs = [r[0].astype(BF16) for r in ckv_refs]
    scores = [_dot_nt(ql, kc) + _dot_nt(qp, kr[0].astype(BF16)) for kc, kr in zip(kcs, kpe_refs)]
    update(jnp.concatenate(scores, axis=1), kcs)

    @pl.when(j == pl.num_programs(1) - 1)
    def _():
        t_new = cn_ref.shape[1]
        pad = jnp.zeros((LANES - t_new, KV_LORA), F32)
        kc = jnp.concatenate([cn_ref[0], pad], axis=0).astype(BF16)
        kp = jnp.concatenate([pn_ref[0], pad[:, :LANES]], axis=0).astype(BF16)
        s = _dot_nt(ql, kc) + _dot_nt(q[:, KV_LORA:], kp)
        r = lax.broadcasted_iota(jnp.int32, (rows, LANES), 0)
        c = lax.broadcasted_iota(jnp.int32, (rows, LANES), 1)
        s = jnp.where(c <= r // MLA_HEADS, s, NEG)
        update(s, [kc])
        o_ref[0] = (acc_ref[...] / jnp.concatenate([l_ref[...]] * (KV_LORA // LANES), axis=1)).astype(o_ref.dtype)


def _decode_attend(q_abs, ckv_new, kpe_new, cache_ckv, cache_kpe, page_table):
    batch, rows, _ = q_abs.shape
    n_pages = page_table.shape[1]
    page = cache_ckv.shape[1]
    g_pages = min(PAGES_PER_STEP, n_pages)
    t_new = ckv_new.shape[1]

    def page_map(g):
        return lambda b, j, pt: (pt[b * n_pages + j * g_pages + g], 0, 0)

    per_b = lambda b, j, pt: (b, 0, 0)
    in_specs = [pl.BlockSpec((1, rows, LAT_W), per_b),
                pl.BlockSpec((1, t_new, KV_LORA), per_b),
                pl.BlockSpec((1, t_new, LANES), per_b)]
    in_specs += [pl.BlockSpec((1, page, KV_LORA), page_map(g)) for g in range(g_pages)]
    in_specs += [pl.BlockSpec((1, page, QK_ROPE), page_map(g)) for g in range(g_pages)]
    grid_spec = pltpu.PrefetchScalarGridSpec(
        num_scalar_prefetch=1,
        grid=(batch, n_pages // g_pages),
        in_specs=in_specs,
        out_specs=pl.BlockSpec((1, rows, KV_LORA), per_b),
        scratch_shapes=[pltpu.VMEM((rows, LANES), F32), pltpu.VMEM((rows, LANES), F32),
                        pltpu.VMEM((rows, KV_LORA), F32)],
    )
    return pl.pallas_call(
        functools.partial(_decode_kernel, g_pages=g_pages),
        out_shape=jax.ShapeDtypeStruct((batch, rows, KV_LORA), BF16),
        grid_spec=grid_spec,
        compiler_params=_cparams(("parallel", "arbitrary")),
        name="mla_decode",
    )(page_table.reshape(-1), q_abs, ckv_new, kpe_new, *([cache_ckv] * g_pages), *([cache_kpe] * g_pages))


def _pad_lanes(v, width):
    return jnp.pad(v, [(0, 0)] * (v.ndim - 1) + [(0, width - v.shape[-1])])


def _ssd_params(i, w_in_a, conv_w, conv_b, dt_bias, a_log, d_skip, g_ssd_norm):
    w = w_in_a[i]
    z0, x0, d0, m0 = 0, D_INNER, D_INNER + CONV_DIM, D_INNER + CONV_DIM + SSD_HEADS
    w_main = jnp.concatenate([w[:, x0:d0], w[:, m0:], w[:, z0:x0]], axis=1).astype(BF16)
    w_dt = _pad_lanes(w[:, d0:m0], LANES).astype(BF16)
    head_of = jnp.arange(D_INNER, dtype=jnp.int32) // SSD_HEAD_DIM
    expand = (jnp.arange(LANES, dtype=jnp.int32)[:, None] == head_of[None, :]).astype(BF16)
    return dict(w_main=w_main, w_dt=w_dt, conv_w=conv_w[i], conv_b=conv_b[i].reshape(1, CONV_DIM),
                dt_bias=_pad_lanes(dt_bias[i].reshape(1, SSD_HEADS), LANES),
                a_log=_pad_lanes(a_log[i].reshape(1, SSD_HEADS), LANES),
                d_x=jnp.repeat(d_skip[i], SSD_HEAD_DIM).reshape(1, D_INNER),
                g_norm=g_ssd_norm[i].reshape(1, D_INNER), expand=expand)


def _mla_shared_params(w_kv_a, w_uk, w_uv):
    half = QK_ROPE // 2
    wa = _pad_lanes(w_kv_a, LAT_W).astype(BF16)
    pe = w_kv_a[:, KV_LORA:]
    wr = _pad_lanes(jnp.concatenate([-pe[:, half:], pe[:, :half]], axis=1), LANES).astype(BF16)
    eye = jnp.eye(QK_ROPE, dtype=F32)
    k_blocks, abs_blocks = [], []
    for h in range(MLA_HEADS):
        kb = jnp.zeros((LAT_W, LANES), F32)
        kb = kb.at[:KV_LORA, :QK_NOPE].set(w_uk[:, h, :])
        kb = kb.at[KV_LORA:KV_LORA + QK_ROPE, QK_NOPE:QK_NOPE + QK_ROPE].set(eye)
        k_blocks.append(kb)
        ab = jnp.zeros((LANES, LAT_W), F32)
        ab = ab.at[:QK_NOPE, :KV_LORA].set(w_uk[:, h, :].T)
        ab = ab.at[QK_NOPE:QK_NOPE + QK_ROPE, KV_LORA:KV_LORA + QK_ROPE].set(eye)
        abs_blocks.append(ab)
    w_k = jnp.concatenate(k_blocks, axis=1).astype(BF16)
    w_v = jnp.pad(w_uv.reshape(KV_LORA, MLA_HEADS * V_HEAD), ((0, LAT_W - KV_LORA), (0, 0))).astype(BF16)
    w_abs = jax.scipy.linalg.block_diag(*abs_blocks).astype(BF16)
    w_uvd = jax.scipy.linalg.block_diag(*[w_uv[:, h, :] for h in range(MLA_HEADS)]).astype(BF16)
    return dict(wa=wa, wr=wr, w_k=w_k, w_v=w_v, w_abs=w_abs, w_uvd=w_uvd)


def _mla_layer_params(i, w_in_b, w_q_b):
    half = QK_ROPE // 2
    w = w_in_b[i]
    w_inb = jnp.concatenate([w[:, Q_LORA:], w[:, :Q_LORA]], axis=1).astype(BF16)
    wq = w_q_b[i]
    wqa = _pad_lanes(wq, LANES).reshape(Q_LORA, MLA_HEADS * LANES).astype(BF16)
    rot = jnp.concatenate([jnp.zeros_like(wq[..., :QK_NOPE]), -wq[..., QK_NOPE + half:], wq[..., QK_NOPE:QK_NOPE + half]],
                          axis=-1)
    wqb = _pad_lanes(rot, LANES).reshape(Q_LORA, MLA_HEADS * LANES).astype(BF16)
    return dict(w_inb=w_inb, wqa=wqa, wqb=wqb)


def _rope_tables(pos):
    half = QK_ROPE // 2
    inv = ROPE_BASE ** (-jnp.arange(half, dtype=F32) / half)
    ang = pos.astype(F32)[:, None] * inv[None, :]
    cos, sin = jnp.cos(ang), jnp.sin(ang)
    n = pos.shape[0]
    one = lambda w: jnp.ones((n, w), F32)
    zero = lambda w: jnp.zeros((n, w), F32)
    tail = LANES - QK_NOPE - QK_ROPE
    cos_q = jnp.concatenate([one(QK_NOPE), cos, cos, one(tail)], axis=1)
    sin_q = jnp.concatenate([zero(QK_NOPE), sin, sin, zero(tail)], axis=1)
    cos_k = jnp.concatenate([cos, cos, one(LANES - QK_ROPE)], axis=1)
    sin_k = jnp.concatenate([sin, sin, zero(LANES - QK_ROPE)], axis=1)
    return cos_q, sin_q, cos_k, sin_k


def _trunk(x, batch, t, sample, mem, ssd_state, mla_cache, tabs, P):
    act = F32 if sample else BF16
    cos_q, sin_q, cos_k, sin_k = tabs
    ssm_out, conv_out = [], []
    ckv = kpe = None
    for l in range(DEPTH):
        k_arr, v_arr, kboff, kcol, vcol = mem(l)
        if l < N_A:
            sp = P["ssd"][l]
            proj = _proj(x, sp["w_main"], g=P["g_pre_mix"][l], out_dtype=act, tn=1024, name=f"in_a{l}")
            dt = _proj(x, sp["w_dt"], g=P["g_pre_mix"][l], out_dtype=F32, name=f"in_dt{l}")
            if sample:
                state_ssm, state_conv = ssd_state
                y, conv_new, h_new = _ssd_sample(proj, dt, state_conv[l], state_ssm[l].reshape(batch, D_INNER, D_STATE),
                                                 sp, batch, t)
            else:
                y, h_new = _ssd_prompt(proj, dt, sp, batch, t)
                conv_new = proj.reshape(batch, t, -1)[:, t - 3:, :CONV_DIM].astype(F32)
            ssm_out.append(h_new.reshape(batch, SSD_HEADS, SSD_HEAD_DIM, D_STATE))
            conv_out.append(conv_new)
            mo = _mem_attend(proj, 3, k_arr, v_arr, kboff, kcol, vcol, batch, t, act, f"mem{l}")
            w_out = P["w_out_a"][l]
            x = _outproj(y, mo, w_out[:D_INNER], w_out[D_INNER:], P["g_post_mix"][l], x, f"out_a{l}")
        else:
            i = l - N_A
            ms, ml = P["mla_shared"], P["mla"][i]
            if i == 0:
                ckv, kpe, lat = _kv_latent(x, P["g_kv_in"], ms["wa"], ms["wr"], P["g_kv_norm"], cos_k, sin_k)
                if not sample:
                    k_all = _proj(lat, ms["w_k"], tn=1024, name="k_heads")
                    v_all = _proj(lat, ms["w_v"], name="v_heads")
            proj = _proj(x, ml["w_inb"], g=P["g_pre_mix"][l], out_dtype=act, name=f"in_b{i}")
            q2 = _proj(proj, ml["wqa"], g=P["g_q_norm"][i], xcol=2, tn=1024,
                       rope=(ml["wqb"], cos_q, sin_q, MLA_SCALE), name=f"q_heads{i}")
            if sample:
                cache_ckv, cache_kpe, page_table = mla_cache
                q_abs = _proj(q2, ms["w_abs"], tn=1536, name=f"q_abs{i}").reshape(batch, t * MLA_HEADS, LAT_W)
                o_lat = _decode_attend(q_abs, ckv.reshape(batch, t, KV_LORA), kpe.reshape(batch, t, LANES),
                                       cache_ckv, cache_kpe, page_table)
                o = _proj(o_lat.reshape(batch * t, MLA_HEADS * KV_LORA), ms["w_uvd"], name=f"o_heads{i}")
            else:
                o = _flash_attend(q2, k_all, v_all, batch, t)
            mo = _mem_attend(proj, 0, k_arr, v_arr, kboff, kcol, vcol, batch, t, act, f"mem{l}")
            w_out = P["w_out_b"][i]
            n_o = MLA_HEADS * V_HEAD
            x = _outproj(o, mo, w_out[:n_o], w_out[n_o:], P["g_post_mix"][l], x, f"out_b{i}")
        x = _ffn(x, P["g_pre_ffn"][l], P["w_ffn_up"][l], P["w_ffn_down"][l], P["g_post_ffn"][l], f"ffn{l}")
    return x, jnp.stack(ssm_out), jnp.stack(conv_out), ckv, kpe[:, :QK_ROPE]


def kernel(x_prompt, x_sample, mem_prompt, state_ssm, state_conv, cache_ckv, cache_kpe, cache_mem_k, cache_mem_v, page_table, g_pre_mix, g_post_mix, g_pre_ffn, g_post_ffn, w_ffn_up, w_ffn_down, w_mem_k, w_mem_v, w_in_a, conv_w, conv_b, dt_bias, a_log, d_skip, g_ssd_norm, w_out_a, g_kv_in, w_kv_a, g_kv_norm, w_uk, w_uv, w_in_b, g_q_norm, w_q_b, w_out_b):
    b, s, _ = x_prompt.shape
    db, t, _ = x_sample.shape
    past_len = page_table.shape[1] * cache_ckv.shape[1]

    P = dict(
        g_pre_mix=g_pre_mix, g_post_mix=g_post_mix, g_pre_ffn=g_pre_ffn, g_post_ffn=g_post_ffn,
        g_kv_in=g_kv_in, g_kv_norm=g_kv_norm, g_q_norm=g_q_norm,
        w_ffn_up=w_ffn_up.astype(BF16), w_ffn_down=w_ffn_down.astype(BF16),
        w_out_a=w_out_a.astype(BF16), w_out_b=w_out_b.astype(BF16),
        ssd=[_ssd_params(i, w_in_a, conv_w, conv_b, dt_bias, a_log, d_skip, g_ssd_norm) for i in range(N_A)],
        mla_shared=_mla_shared_params(w_kv_a, w_uk, w_uv),
        mla=[_mla_layer_params(i, w_in_b, w_q_b) for i in range(DEPTH - N_A)],
    )

    w_mem = jnp.concatenate([w_mem_k[l] for l in range(DEPTH)] + [w_mem_v[l] for l in range(DEPTH)], axis=1)
    mem_kv = _proj(mem_prompt.reshape(b * N_MEM, D_MODEL), w_mem.astype(BF16), out_dtype=F32, tn=1024, name="mem_kv")
    mem_kv5 = mem_kv.reshape(b, N_MEM, 2 * DEPTH, MEM_HEADS, MEM_HEAD_DIM)
    p_mem_k = jnp.transpose(mem_kv5[:, :, :DEPTH], (2, 0, 1, 3, 4))
    p_mem_v = jnp.transpose(mem_kv5[:, :, DEPTH:], (2, 0, 1, 3, 4))
    mem_kv3 = mem_kv.reshape(b, N_MEM, 2 * DEPTH * MEM_WIDTH)
    tabs_p = _rope_tables(jnp.arange(s, dtype=jnp.int32))
    y_prompt, p_ssm, p_conv, p_ckv, p_kpe = _trunk(
        x_prompt.reshape(b * s, D_MODEL), b, s, False,
        lambda l: (mem_kv3, mem_kv3, 0, l, DEPTH + l), None, None, tabs_p, P)

    mk = cache_mem_k.reshape(DEPTH * db, N_MEM, MEM_WIDTH)
    mv = cache_mem_v.reshape(DEPTH * db, N_MEM, MEM_WIDTH)
    tabs_s = tuple(jnp.tile(tb, (db, 1)) for tb in _rope_tables(past_len + jnp.arange(t, dtype=jnp.int32)))
    y_sample, s_ssm, s_conv, s_ckv, s_kpe = _trunk(
        x_sample.reshape(db * t, D_MODEL), db, t, True,
        lambda l: (mk, mv, l * db, 0, 0), (state_ssm, state_conv), (cache_ckv, cache_kpe, page_table), tabs_s, P)

    return (y_prompt.reshape(b, s, D_MODEL), y_sample.reshape(db, t, D_MODEL),
            p_ssm, p_conv, p_ckv.reshape(b, s, KV_LORA), p_kpe.reshape(b, s, QK_ROPE), p_mem_k, p_mem_v,
            s_ssm, s_conv, s_ckv.reshape(db, t, KV_LORA), s_kpe.reshape(db, t, QK_ROPE))
```

```python
import functools

import jax
import jax.numpy as jnp
from jax import lax
from jax.experimental import pallas as pl
from jax.experimental.pallas import tpu as pltpu

F32 = jnp.float32
BF16 = jnp.bfloat16

D_MODEL = 1024
DEPTH = 4
N_A = 2
D_INNER = 2048
SSD_HEADS = 32
SSD_HEAD_DIM = 64
SSD_GROUPS = 4
GROUP_W = D_INNER // SSD_GROUPS
D_STATE = 128
CONV_DIM = 3072
SSD_CHUNK = 128
MLA_HEADS = 16
Q_LORA = 512
KV_LORA = 256
QK_NOPE = 64
QK_ROPE = 32
V_HEAD = 64
ROPE_BASE = 10000.0
MLA_SCALE = (QK_NOPE + QK_ROPE) ** -0.5
N_MEM = 256
MEM_HEADS = 4
MEM_HEAD_DIM = 256
MEM_WIDTH = 1024
D_FF = 4096
RMS_EPS = 1e-6
LANES = 128
LAT_W = 384
NEG = -1e30
ROW_TILE = 512
ATT_TILE = 1024
ATT_SUB = 512
DECODE_CHUNK = 2048
MEM_BATCH_BLOCK = 4

NT_DIMS = (((1,), (1,)), ((), ()))
TN_DIMS = (((0,), (0,)), ((), ()))


def _cparams(sem, vmem_mb=48):
    return pltpu.CompilerParams(dimension_semantics=sem, vmem_limit_bytes=vmem_mb * 1024 * 1024)


def _rms(x, g):
    ms = jnp.mean(x * x, axis=-1, keepdims=True)
    return x * lax.rsqrt(ms + RMS_EPS) * g


def _silu(x):
    return x * (1.0 / (1.0 + jnp.exp(-x)))


def _softplus(x):
    return jnp.maximum(x, 0.0) + jnp.log1p(jnp.exp(-jnp.abs(x)))


def _dot(a, b):
    return jnp.dot(a, b, preferred_element_type=F32)


def _dot_nt(a, b):
    return lax.dot_general(a, b, NT_DIMS, preferred_element_type=F32)


def _dot_tn(a, b):
    return lax.dot_general(a, b, TN_DIMS, preferred_element_type=F32)


def _split_bf16(v, terms):
    parts = []
    for _ in range(terms):
        p = v.astype(BF16)
        parts.append(p)
        v = v - p.astype(F32)
    return parts


def _sel_dot(m, v, terms=2):
    return sum(_dot(m, p) for p in _split_bf16(v, terms))


def _expand(v, e_ref, terms=2):
    return sum(_dot(p, e_ref[...]) for p in _split_bf16(v, terms))


def _proj_kernel(*refs, has_g, has_rope, scale):
    it = iter(refs)
    x_ref = next(it)
    g_ref = next(it) if has_g else None
    w_ref = next(it)
    if has_rope:
        w2_ref, cos_ref, sin_ref = next(it), next(it), next(it)
    o_ref = next(it)
    xn_ref = next(it)

    @pl.when(pl.program_id(1) == 0)
    def _():
        x = x_ref[...].astype(F32)
        if has_g:
            x = _rms(x, g_ref[...])
        xn_ref[...] = x.astype(BF16)

    xn = xn_ref[...]
    y = _dot(xn, w_ref[...])
    if has_rope:
        rep = y.shape[1] // LANES
        cos = jnp.concatenate([cos_ref[...]] * rep, axis=1)
        sin = jnp.concatenate([sin_ref[...]] * rep, axis=1)
        y = (y * cos + _dot(xn, w2_ref[...]) * sin) * scale
    o_ref[...] = y.astype(o_ref.dtype)


def _proj(x, w, *, g=None, xcol=0, out_dtype=BF16, tn=None, rope=None, row_tiles=1, name):
    n = x.shape[0]
    k, nout = w.shape
    tm = min(ROW_TILE * row_tiles, n)
    tn = nout if tn is None else tn
    in_specs = [pl.BlockSpec((tm, k), lambda i, j: (i, xcol))]
    args = [x]
    if g is not None:
        in_specs.append(pl.BlockSpec((1, k), lambda i, j: (0, 0)))
        args.append(g.reshape(1, k).astype(F32))
    in_specs.append(pl.BlockSpec((k, tn), lambda i, j: (0, j)))
    args.append(w)
    scale = 1.0
    if rope is not None:
        w2, cos, sin, scale = rope
        nb = cos.shape[0] // tm
        in_specs.append(pl.BlockSpec((k, tn), lambda i, j: (0, j)))
        in_specs.append(pl.BlockSpec((tm, LANES), lambda i, j: (i % nb, 0)))
        in_specs.append(pl.BlockSpec((tm, LANES), lambda i, j: (i % nb, 0)))
        args += [w2, cos, sin]
    return pl.pallas_call(
        functools.partial(_proj_kernel, has_g=g is not None, has_rope=rope is not None, scale=scale),
        out_shape=jax.ShapeDtypeStruct((n, nout), out_dtype),
        grid=(n // tm, nout // tn),
        in_specs=in_specs,
        out_specs=pl.BlockSpec((tm, tn), lambda i, j: (i, j)),
        scratch_shapes=[pltpu.VMEM((tm, k), BF16)],
        compiler_params=_cparams(("parallel", "arbitrary")),
        name=name,
    )(*args)


def _block_proj_kernel(x_ref, w_ref, o_ref):
    o_ref[...] = _dot(x_ref[...].astype(BF16), w_ref[0]).astype(o_ref.dtype)


def _block_proj(x, w, name):
    n = x.shape[0]
    nb, kx, ko = w.shape
    tm = min(ROW_TILE, n)
    return pl.pallas_call(
        _block_proj_kernel,
        out_shape=jax.ShapeDtypeStruct((n, nb * ko), BF16),
        grid=(n // tm, nb),
        in_specs=[pl.BlockSpec((tm, kx), lambda i, j: (i, j)), pl.BlockSpec((1, kx, ko), lambda i, j: (j, 0, 0))],
        out_specs=pl.BlockSpec((tm, ko), lambda i, j: (i, j)),
        compiler_params=_cparams(("parallel", "parallel")),
        name=name,
    )(x, w)


def _kvlat_kernel(x_ref, g_ref, wa_ref, wr_ref, gkv_ref, cos_ref, sin_ref, ckv_ref, kpe_ref, lat_ref):
    xn = _rms(x_ref[...], g_ref[...]).astype(BF16)
    a = _dot(xn, wa_ref[...])
    r = _dot(xn, wr_ref[...])
    ckv = _rms(a[:, :KV_LORA], gkv_ref[...])
    pe = a[:, KV_LORA:] * cos_ref[...] + r * sin_ref[...]
    ckv_ref[...] = ckv
    kpe_ref[...] = pe
    lat_ref[:, :KV_LORA] = ckv.astype(BF16)
    lat_ref[:, KV_LORA:] = pe.astype(BF16)


def _kv_latent(x, g, wa, wr, gkv, cos, sin):
    n = x.shape[0]
    tm = min(ROW_TILE, n)
    nb = cos.shape[0] // tm
    row = lambda i: (i, 0)
    const = lambda i: (0, 0)
    return pl.pallas_call(
        _kvlat_kernel,
        out_shape=(jax.ShapeDtypeStruct((n, KV_LORA), F32),
                   jax.ShapeDtypeStruct((n, LANES), F32),
                   jax.ShapeDtypeStruct((n, LAT_W), BF16)),
        grid=(n // tm,),
        in_specs=[pl.BlockSpec((tm, D_MODEL), row),
                  pl.BlockSpec((1, D_MODEL), const),
                  pl.BlockSpec((D_MODEL, LAT_W), const),
                  pl.BlockSpec((D_MODEL, LANES), const),
                  pl.BlockSpec((1, KV_LORA), const),
                  pl.BlockSpec((tm, LANES), lambda i: (i % nb, 0)),
                  pl.BlockSpec((tm, LANES), lambda i: (i % nb, 0))],
        out_specs=(pl.BlockSpec((tm, KV_LORA), row),
                   pl.BlockSpec((tm, LANES), row),
                   pl.BlockSpec((tm, LAT_W), row)),
        compiler_params=_cparams(("parallel",)),
        name="kv_latent",
    )(x, g.reshape(1, D_MODEL), wa, wr, gkv.reshape(1, KV_LORA), cos, sin)


def _outproj_kernel(a1_ref, a2_ref, w1_ref, w2_ref, g_ref, r_ref, o_ref):
    acc = _dot(a1_ref[...].astype(BF16), w1_ref[...]) + _dot(a2_ref[...].astype(BF16), w2_ref[...])
    o_ref[...] = r_ref[...] + _rms(acc, g_ref[...])


def _outproj(a1, a2, w1, w2, g, res, name):
    n = res.shape[0]
    tm = min(ROW_TILE, n)
    k1, k2 = w1.shape[0], w2.shape[0]
    row = lambda i: (i, 0)
    const = lambda i: (0, 0)
    return pl.pallas_call(
        _outproj_kernel,
        out_shape=jax.ShapeDtypeStruct((n, D_MODEL), F32),
        grid=(n // tm,),
        in_specs=[pl.BlockSpec((tm, k1), row), pl.BlockSpec((tm, k2), row),
                  pl.BlockSpec((k1, D_MODEL), const), pl.BlockSpec((k2, D_MODEL), const),
                  pl.BlockSpec((1, D_MODEL), const), pl.BlockSpec((tm, D_MODEL), row)],
        out_specs=pl.BlockSpec((tm, D_MODEL), row),
        compiler_params=_cparams(("parallel",)),
        name=name,
    )(a1, a2, w1, w2, g.reshape(1, D_MODEL), res)


def _ffn_kernel(x_ref, g1_ref, wu_ref, wd_ref, g2_ref, o_ref):
    x = x_ref[...]
    xn = _rms(x, g1_ref[...]).astype(BF16)
    acc = jnp.zeros(x.shape, F32)
    for c in range(D_FF // D_MODEL):
        sl = slice(c * D_MODEL, (c + 1) * D_MODEL)
        h = _dot(xn, wu_ref[:, sl])
        h = jnp.square(jnp.maximum(h, 0.0)).astype(BF16)
        acc = acc + _dot(h, wd_ref[sl, :])
    o_ref[...] = x + _rms(acc, g2_ref[...])


def _ffn(x, g1, wu, wd, g2, name):
    n = x.shape[0]
    tm = min(ROW_TILE, n)
    row = lambda i: (i, 0)
    const = lambda i: (0, 0)
    return pl.pallas_call(
        _ffn_kernel,
        out_shape=jax.ShapeDtypeStruct((n, D_MODEL), F32),
        grid=(n // tm,),
        in_specs=[pl.BlockSpec((tm, D_MODEL), row), pl.BlockSpec((1, D_MODEL), const),
                  pl.BlockSpec((D_MODEL, D_FF), const, pipeline_mode=pl.Buffered(1)),
                  pl.BlockSpec((D_FF, D_MODEL), const, pipeline_mode=pl.Buffered(1)),
                  pl.BlockSpec((1, D_MODEL), const)],
        out_specs=pl.BlockSpec((tm, D_MODEL), row),
        compiler_params=_cparams(("parallel",), vmem_mb=48),
        name=name,
    )(x, g1.reshape(1, D_MODEL), wu, wd, g2.reshape(1, D_MODEL))


def _mem_head(qh, kh, vh):
    s = _dot_nt(qh.astype(BF16), kh.astype(BF16)) * (MEM_HEAD_DIM ** -0.5)
    m = jnp.max(s, axis=-1, keepdims=True)
    p = jnp.exp(s - m)
    l = jnp.sum(p, axis=-1, keepdims=True)
    return _dot(p.astype(BF16), vh.astype(BF16)) / l


def _mem_kernel(q_ref, k_ref, v_ref, o_ref):
    q = q_ref[...]
    for h in range(MEM_HEADS):
        sl = slice(h * MEM_HEAD_DIM, (h + 1) * MEM_HEAD_DIM)
        o_ref[:, sl] = _mem_head(q[:, sl], k_ref[0, :, sl], v_ref[0, :, sl]).astype(o_ref.dtype)


def _mem_cache_kernel(q_ref, k_hbm, v_hbm, o_ref, kbuf, vbuf, sem, *, layer, bb, t):
    step = pl.program_id(0)
    slot = step % 2

    def copies(st, sl):
        out = []
        for i in range(bb):
            for h in range(MEM_HEADS):
                out.append(pltpu.make_async_copy(k_hbm.at[layer, st * bb + i, :, h, :], kbuf.at[sl, i, h], sem.at[0, sl]))
                out.append(pltpu.make_async_copy(v_hbm.at[layer, st * bb + i, :, h, :], vbuf.at[sl, i, h], sem.at[1, sl]))
        return out

    @pl.when(step == 0)
    def _():
        for cp in copies(0, 0):
            cp.start()

    @pl.when(step + 1 < pl.num_programs(0))
    def _():
        for cp in copies(step + 1, 1 - slot):
            cp.start()

    for cp in copies(step, slot):
        cp.wait()

    for i in range(bb):
        rows = slice(i * t, (i + 1) * t)
        for h in range(MEM_HEADS):
            sl = slice(h * MEM_HEAD_DIM, (h + 1) * MEM_HEAD_DIM)
            o_ref[rows, sl] = _mem_head(q_ref[rows, sl], kbuf[slot, i, h], vbuf[slot, i, h]).astype(o_ref.dtype)


def _mem_attend_cache(q, qcol, cache_k, cache_v, layer, batch, t, name):
    bb = min(MEM_BATCH_BLOCK, batch)
    buf = pltpu.VMEM((2, bb, MEM_HEADS, N_MEM, MEM_HEAD_DIM), F32)
    return pl.pallas_call(
        functools.partial(_mem_cache_kernel, layer=layer, bb=bb, t=t),
        out_shape=jax.ShapeDtypeStruct((batch * t, MEM_WIDTH), F32),
        grid=(batch // bb,),
        in_specs=[pl.BlockSpec((bb * t, MEM_WIDTH), lambda i: (i, qcol)),
                  pl.BlockSpec(memory_space=pl.ANY), pl.BlockSpec(memory_space=pl.ANY)],
        out_specs=pl.BlockSpec((bb * t, MEM_WIDTH), lambda i: (i, 0)),
        scratch_shapes=[buf, buf, pltpu.SemaphoreType.DMA((2, 2))],
        compiler_params=_cparams(("arbitrary",)),
        name=name,
    )(q, cache_k, cache_v)


def _mem_attend(q, qcol, k_arr, v_arr, kboff, kcol, vcol, batch, t, out_dtype, name):
    tq = min(ROW_TILE, t)
    nt = t // tq
    return pl.pallas_call(
        _mem_kernel,
        out_shape=jax.ShapeDtypeStruct((batch * t, MEM_WIDTH), out_dtype),
        grid=(batch, nt),
        in_specs=[pl.BlockSpec((tq, MEM_WIDTH), lambda b, i: (b * nt + i, qcol)),
                  pl.BlockSpec((1, N_MEM, MEM_WIDTH), lambda b, i: (b + kboff, 0, kcol)),
                  pl.BlockSpec((1, N_MEM, MEM_WIDTH), lambda b, i: (b + kboff, 0, vcol))],
        out_specs=pl.BlockSpec((tq, MEM_WIDTH), lambda b, i: (b * nt + i, 0)),
        compiler_params=_cparams(("parallel", "parallel")),
        name=name,
    )(q, k_arr, v_arr)


def _gate_norm_store(y_ref, ygrp, z_ref, gn_ref, g):
    sl = slice(g * GROUP_W, (g + 1) * GROUP_W)
    ygrp = ygrp * _silu(z_ref[:, sl].astype(F32))
    y_ref[:, sl] = _rms(ygrp, gn_ref[:, sl]).astype(y_ref.dtype)


def _ssd_prompt_kernel(xs_ref, bc_ref, z_ref, dt_ref, cw_ref, cb_ref, dtb_ref, alog_ref, d_ref, gn_ref, e_ref,
                       y_ref, hout_ref, extx, extbc, ht):
    q = SSD_CHUNK
    c = pl.program_id(1)

    @pl.when(c == 0)
    def _():
        extx[0:8] = jnp.zeros((8, D_INNER), F32)
        extbc[0:8] = jnp.zeros((8, CONV_DIM - D_INNER), F32)
        ht[...] = jnp.zeros(ht.shape, F32)

    @pl.when(c > 0)
    def _():
        extx[0:8] = extx[q:q + 8]
        extbc[0:8] = extbc[q:q + 8]

    extx[8:q + 8] = xs_ref[...].astype(F32)
    extbc[8:q + 8] = bc_ref[...].astype(F32)

    def conv(ext, w, b):
        acc = b + w[0:1] * ext[5:q + 5] + w[1:2] * ext[6:q + 6] + w[2:3] * ext[7:q + 7] + w[3:4] * ext[8:q + 8]
        return _silu(acc)

    cw = cw_ref[...]
    cb = cb_ref[...]
    xs = conv(extx, cw[:, :D_INNER], cb[:, :D_INNER])
    bcv = conv(extbc, cw[:, D_INNER:], cb[:, D_INNER:])

    dt = _softplus(dt_ref[...] + dtb_ref[...])
    a = dt * (-jnp.exp(alog_ref[...]))
    row = lax.broadcasted_iota(jnp.int32, (q, q), 0)
    col = lax.broadcasted_iota(jnp.int32, (q, q), 1)
    causal = row >= col
    acs = _sel_dot(jnp.where(causal, 1.0, 0.0).astype(BF16), a, terms=3)
    acs_t = acs.T
    a_last = acs[q - 1:q, :]
    dtx = _expand(dt, e_ref)
    dwx = _expand(dt * jnp.exp(a_last - acs), e_ref)
    eax = _expand(jnp.exp(acs), e_ref)
    xdt = xs * dtx
    xw_b = (xs * dwx).astype(BF16)
    lane = lax.broadcasted_iota(jnp.int32, (q, LANES), 1)
    heads_per_group = SSD_HEADS // SSD_GROUPS

    for g in range(SSD_GROUPS):
        gs = slice(g * GROUP_W, (g + 1) * GROUP_W)
        bg = bcv[:, g * D_STATE:(g + 1) * D_STATE].astype(BF16)
        cg = bcv[:, (SSD_GROUPS + g) * D_STATE:(SSD_GROUPS + g + 1) * D_STATE].astype(BF16)
        cbm = _dot_nt(cg, bg)
        htg = ht[:, gs]
        y_off = _dot(cg, htg.astype(BF16)) * eax[:, gs]
        ht[:, gs] = eax[q - 1:q, gs] * htg + _dot_tn(bg, xw_b[:, gs])
        pairs = []
        for j in range(heads_per_group // 2):
            h0 = g * heads_per_group + 2 * j
            xpair = xdt[:, h0 * SSD_HEAD_DIM:(h0 + 2) * SSD_HEAD_DIM]
            yp = None
            for k in range(2):
                h = h0 + k
                seg = acs[:, h:h + 1] - acs_t[h:h + 1, :]
                m = (cbm * jnp.exp(jnp.where(causal, seg, NEG))).astype(BF16)
                keep = (lane < SSD_HEAD_DIM) if k == 0 else (lane >= SSD_HEAD_DIM)
                t = _dot(m, jnp.where(keep, xpair, 0.0).astype(BF16))
                yp = t if yp is None else yp + t
            pairs.append(yp)
        ygrp = jnp.concatenate(pairs, axis=1) + y_off + d_ref[:, gs] * xs[:, gs]
        _gate_norm_store(y_ref, ygrp, z_ref, gn_ref, g)

    @pl.when(c == pl.num_programs(1) - 1)
    def _():
        hout_ref[0] = ht[...].T


def _ssd_prompt(proj, dt, sp, batch, t):
    nc = t // SSD_CHUNK
    q = SSD_CHUNK
    const = lambda b, c: (0, 0)
    return pl.pallas_call(
        _ssd_prompt_kernel,
        out_shape=(jax.ShapeDtypeStruct((batch * t, D_INNER), BF16),
                   jax.ShapeDtypeStruct((batch, D_INNER, D_STATE), F32)),
        grid=(batch, nc),
        in_specs=[pl.BlockSpec((q, D_INNER), lambda b, c: (b * nc + c, 0)),
                  pl.BlockSpec((q, 1024), lambda b, c: (b * nc + c, 2)),
                  pl.BlockSpec((q, D_INNER), lambda b, c: (b * nc + c, 2)),
                  pl.BlockSpec((q, LANES), lambda b, c: (b * nc + c, 0)),
                  pl.BlockSpec((4, CONV_DIM), const), pl.BlockSpec((1, CONV_DIM), const),
                  pl.BlockSpec((1, LANES), const), pl.BlockSpec((1, LANES), const),
                  pl.BlockSpec((1, D_INNER), const), pl.BlockSpec((1, D_INNER), const),
                  pl.BlockSpec((LANES, D_INNER), const)],
        out_specs=(pl.BlockSpec((q, D_INNER), lambda b, c: (b * nc + c, 0)),
                   pl.BlockSpec((1, D_INNER, D_STATE), lambda b, c: (b, 0, 0))),
        scratch_shapes=[pltpu.VMEM((q + 8, D_INNER), F32),
                        pltpu.VMEM((q + 8, CONV_DIM - D_INNER), F32),
                        pltpu.VMEM((D_STATE, D_INNER), F32)],
        compiler_params=_cparams(("parallel", "arbitrary"), vmem_mb=48),
        name="ssd_prompt",
    )(proj, proj, proj, dt, sp["conv_w"], sp["conv_b"], sp["dt_bias"], sp["a_log"], sp["d_x"], sp["g_norm"], sp["expand"])


def _ssd_sample_kernel(xs_ref, bc_ref, z_ref, dt_ref, cs_ref, h0_ref, cw_ref, cb_ref, dtb_ref, alog_ref, d_ref,
                       gn_ref, e_ref, *rest, n_carried, out_layer):
    y_ref, cnew_ref, hnew_ref, ext = rest[n_carried:]
    t = xs_ref.shape[0]
    for other in range(cnew_ref.shape[0]):
        if other != out_layer:
            cnew_ref[other] = jnp.zeros(cnew_ref.shape[1:], F32)
            hnew_ref[other] = jnp.zeros(hnew_ref.shape[1:], F32)
    ext[5:8] = cs_ref[0]
    ext[8:8 + t, 0:D_INNER] = xs_ref[...]
    ext[8:8 + t, D_INNER:CONV_DIM] = bc_ref[...]
    cw = cw_ref[...]
    acc = (cb_ref[...] + cw[0:1] * ext[5:5 + t] + cw[1:2] * ext[6:6 + t]
           + cw[2:3] * ext[7:7 + t] + cw[3:4] * ext[8:8 + t])
    cnew_ref[out_layer, 0] = ext[5 + t:8 + t]
    xbc = _silu(acc)
    xs = xbc[:, :D_INNER]

    dt = _softplus(dt_ref[...] + dtb_ref[...])
    a = dt * (-jnp.exp(alog_ref[...]))
    rowi = lax.broadcasted_iota(jnp.int32, (t, LANES), 0)
    acs = jnp.zeros((t, LANES), F32)
    for s in range(t):
        acs = acs + jnp.where(rowi >= s, a[s:s + 1, :], 0.0)
    a_last = acs[t - 1:t, :]
    eacs = jnp.exp(acs)
    acsx = _expand(acs, e_ref, terms=3)
    dtx = _expand(dt, e_ref)
    dwx = _expand(dt * jnp.exp(a_last - acs), e_ref)
    eax = _expand(eacs, e_ref)
    xdt = xs * dtx
    xw_b = (xs * dwx).astype(BF16)

    def grp(base, g):
        return xbc[:, D_INNER + (base + g) * D_STATE:D_INNER + (base + g + 1) * D_STATE]

    li = lax.broadcasted_iota(jnp.int32, (t, D_INNER), 0)
    y = jnp.zeros((t, D_INNER), F32)
    for s in range(t):
        decay = jnp.exp(jnp.where(li >= s, acsx - acsx[s:s + 1, :], NEG))
        cbx = jnp.concatenate(
            [jnp.broadcast_to(jnp.sum(grp(SSD_GROUPS, g) * grp(0, g)[s:s + 1, :], axis=-1, keepdims=True),
                              (t, GROUP_W)) for g in range(SSD_GROUPS)], axis=1)
        y = y + cbx * decay * xdt[s:s + 1, :]

    last_only = jnp.where(rowi == t - 1, eacs, 0.0)
    ones = jnp.ones((t, LANES), BF16)
    rdec = sum(_dot_tn(p, ones) for p in _split_bf16(last_only, 3))

    heads_per_group = SSD_HEADS // SSD_GROUPS
    for g in range(SSD_GROUPS):
        gs = slice(g * GROUP_W, (g + 1) * GROUP_W)
        bg = grp(0, g).astype(BF16)
        cg = grp(SSD_GROUPS, g).astype(BF16)
        h0g = h0_ref[0, gs, :]
        y_off = _dot_nt(cg, h0g.astype(BF16)) * eax[:, gs]
        upd = _dot_tn(xw_b[:, gs], bg)
        for hh in range(heads_per_group):
            h = g * heads_per_group + hh
            rs = slice(hh * SSD_HEAD_DIM, (hh + 1) * SSD_HEAD_DIM)
            hnew_ref[out_layer, 0, h * SSD_HEAD_DIM:(h + 1) * SSD_HEAD_DIM, :] = (
                rdec[h:h + 1, :] * h0g[rs, :] + upd[rs, :])
        ygrp = y[:, gs] + y_off + d_ref[:, gs] * xs[:, gs]
        _gate_norm_store(y_ref, ygrp, z_ref, gn_ref, g)


def _ssd_sample(proj, dt, conv_all, ssm_all, stacked, layer, sp, batch, t):
    const = lambda b: (0, 0)
    n_layers = ssm_all.shape[0]
    conv_spec = pl.BlockSpec((None, 1, 3, CONV_DIM), lambda b: (layer, b, 0, 0))
    ssm_spec = pl.BlockSpec((None, 1, D_INNER, D_STATE), lambda b: (layer, b, 0, 0))
    in_specs = [pl.BlockSpec((t, D_INNER), lambda b: (b, 0)),
                pl.BlockSpec((t, 1024), lambda b: (b, 2)),
                pl.BlockSpec((t, D_INNER), lambda b: (b, 2)),
                pl.BlockSpec((t, LANES), lambda b: (b, 0)),
                conv_spec, ssm_spec,
                pl.BlockSpec((4, CONV_DIM), const), pl.BlockSpec((1, CONV_DIM), const),
                pl.BlockSpec((1, LANES), const), pl.BlockSpec((1, LANES), const),
                pl.BlockSpec((1, D_INNER), const), pl.BlockSpec((1, D_INNER), const),
                pl.BlockSpec((LANES, D_INNER), const)]
    args = [proj, proj, proj, dt, conv_all, ssm_all, sp["conv_w"], sp["conv_b"], sp["dt_bias"], sp["a_log"],
            sp["d_x"], sp["g_norm"], sp["expand"]]
    if stacked is None:
        aliases = {}
        out_layer = layer
        conv_out = pl.BlockSpec((n_layers, 1, 3, CONV_DIM), lambda b: (0, b, 0, 0))
        ssm_out = pl.BlockSpec((n_layers, 1, D_INNER, D_STATE), lambda b: (0, b, 0, 0))
    else:
        aliases = {len(args): 1, len(args) + 1: 2}
        in_specs += [pl.BlockSpec(memory_space=pl.ANY), pl.BlockSpec(memory_space=pl.ANY)]
        args += list(stacked)
        out_layer = 0
        conv_out = pl.BlockSpec((1, 1, 3, CONV_DIM), lambda b: (layer, b, 0, 0))
        ssm_out = pl.BlockSpec((1, 1, D_INNER, D_STATE), lambda b: (layer, b, 0, 0))
    y, conv_new, ssm_new = pl.pallas_call(
        functools.partial(_ssd_sample_kernel, n_carried=len(aliases), out_layer=out_layer),
        out_shape=(jax.ShapeDtypeStruct((batch * t, D_INNER), F32),
                   jax.ShapeDtypeStruct((n_layers, batch, 3, CONV_DIM), F32),
                   jax.ShapeDtypeStruct((n_layers, batch, D_INNER, D_STATE), F32)),
        grid=(batch,),
        in_specs=in_specs,
        out_specs=(pl.BlockSpec((t, D_INNER), lambda b: (b, 0)), conv_out, ssm_out),
        scratch_shapes=[pltpu.VMEM((8 + t, CONV_DIM), F32)],
        input_output_aliases=aliases,
        compiler_params=_cparams(("parallel",)),
        name="ssd_sample",
    )(*args)
    return y, (conv_new, ssm_new)


def _flash_kernel(qt_ref, kt_ref, q_ref, k_ref, v_ref, o_ref, m_ref, l_ref, acc_ref, *, sub):
    step = pl.program_id(2)
    qi = qt_ref[step]
    ki = kt_ref[step]
    nsub = q_ref.shape[0] // sub
    first = lax.broadcasted_iota(jnp.int32, (sub, LANES), 1) < V_HEAD

    @pl.when(ki == 0)
    def _():
        m_ref[...] = jnp.full(m_ref.shape, NEG, F32)
        l_ref[...] = jnp.zeros(l_ref.shape, F32)
        acc_ref[...] = jnp.zeros(acc_ref.shape, F32)

    def unit(qs, ks, masked):
        rows = slice(qs * sub, (qs + 1) * sub)
        cols = slice(ks * sub, (ks + 1) * sub)
        vf = v_ref[cols, :].astype(F32)
        pv = None
        alphas = []
        for h in range(2):
            hs = slice(h * LANES, (h + 1) * LANES)
            s = _dot_nt(q_ref[rows, hs], k_ref[cols, hs])
            if masked:
                r = lax.broadcasted_iota(jnp.int32, (sub, sub), 0)
                c = lax.broadcasted_iota(jnp.int32, (sub, sub), 1)
                s = jnp.where(r >= c, s, NEG)
            m_prev = m_ref[h, rows, :]
            m_new = jnp.maximum(m_prev, jnp.max(s, axis=-1, keepdims=True))
            p = jnp.exp(s - jnp.concatenate([m_new] * (sub // LANES), axis=1))
            alpha = jnp.exp(m_prev - m_new)
            l_ref[h, rows, :] = alpha * l_ref[h, rows, :] + jnp.sum(p, axis=-1, keepdims=True)
            m_ref[h, rows, :] = m_new
            vh = jnp.where(first if h == 0 else jnp.logical_not(first), vf, 0.0).astype(BF16)
            t = _dot(p.astype(BF16), vh)
            pv = t if pv is None else pv + t
            alphas.append(alpha)
        acc_ref[rows, :] = acc_ref[rows, :] * jnp.where(first, alphas[0], alphas[1]) + pv

    @pl.when(ki < qi)
    def _():
        for qs in range(nsub):
            for ks in range(nsub):
                unit(qs, ks, False)

    @pl.when(ki == qi)
    def _():
        for qs in range(nsub):
            for ks in range(qs + 1):
                unit(qs, ks, ks == qs)
        first_all = lax.broadcasted_iota(jnp.int32, acc_ref.shape, 1) < V_HEAD
        o_ref[...] = (acc_ref[...] / jnp.where(first_all, l_ref[0], l_ref[1])).astype(o_ref.dtype)


def _flash_attend(q2, k_all, v_all, batch, t):
    tq = min(ATT_TILE, t)
    nq = t // tq
    pairs = [(qi, ki) for qi in range(nq) for ki in range(qi + 1)]
    qt = jnp.asarray([p[0] for p in pairs], jnp.int32)
    kt = jnp.asarray([p[1] for p in pairs], jnp.int32)
    grid_spec = pltpu.PrefetchScalarGridSpec(
        num_scalar_prefetch=2,
        grid=(batch, MLA_HEADS // 2, len(pairs)),
        in_specs=[pl.BlockSpec((tq, 2 * LANES), lambda b, hp, s, qt, kt: (b * nq + qt[s], hp)),
                  pl.BlockSpec((tq, 2 * LANES), lambda b, hp, s, qt, kt: (b * nq + kt[s], hp)),
                  pl.BlockSpec((tq, 2 * V_HEAD), lambda b, hp, s, qt, kt: (b * nq + kt[s], hp))],
        out_specs=pl.BlockSpec((tq, 2 * V_HEAD), lambda b, hp, s, qt, kt: (b * nq + qt[s], hp)),
        scratch_shapes=[pltpu.VMEM((2, tq, LANES), F32), pltpu.VMEM((2, tq, LANES), F32),
                        pltpu.VMEM((tq, LANES), F32)],
    )
    return pl.pallas_call(
        functools.partial(_flash_kernel, sub=min(ATT_SUB, tq)),
        out_shape=jax.ShapeDtypeStruct((batch * t, MLA_HEADS * V_HEAD), BF16),
        grid_spec=grid_spec,
        compiler_params=_cparams(("parallel", "parallel", "arbitrary")),
        name="mla_flash",
    )(qt, kt, q2, k_all, v_all)


def _decode_kernel(pt_ref, q_ref, cn_ref, pn_ref, ckv_hbm, kpe_hbm, o_ref, ckv_buf, kpe_buf, sem, m_ref, l_ref,
                   acc_ref, *, n_pages, page, chunk):
    b = pl.program_id(0)
    slot = b % 2
    rows = q_ref.shape[1]

    def page_copies(bb, sl, p):
        pid = pt_ref[bb * n_pages + p]
        off = pl.multiple_of(p * page, page)
        return (pltpu.make_async_copy(ckv_hbm.at[pid], ckv_buf.at[sl, pl.ds(off, page), :], sem.at[0, sl]),
                pltpu.make_async_copy(kpe_hbm.at[pid], kpe_buf.at[sl, :, pl.ds(off, page)], sem.at[1, sl]))

    def start_all(bb, sl):
        def body(p, carry):
            for cp in page_copies(bb, sl, p):
                cp.start()
            return carry
        lax.fori_loop(0, n_pages, body, 0)

    @pl.when(b == 0)
    def _():
        start_all(0, 0)

    @pl.when(b + 1 < pl.num_programs(0))
    def _():
        start_all(b + 1, 1 - slot)

    def wait_body(p, carry):
        for cp in page_copies(b, slot, p):
            cp.wait()
        return carry
    lax.fori_loop(0, n_pages, wait_body, 0)

    m_ref[...] = jnp.full(m_ref.shape, NEG, F32)
    l_ref[...] = jnp.zeros(l_ref.shape, F32)
    acc_ref[...] = jnp.zeros(acc_ref.shape, F32)

    q = q_ref[0]
    ql = q[:, :KV_LORA]
    qp = q[:, KV_LORA:KV_LORA + QK_ROPE]

    def update(s, values):
        m_prev = m_ref[...]
        m_new = jnp.maximum(m_prev, jnp.max(s, axis=-1, keepdims=True))
        p = jnp.exp(s - jnp.concatenate([m_new] * (s.shape[1] // LANES), axis=1))
        alpha = jnp.exp(m_prev - m_new)
        l_ref[...] = alpha * l_ref[...] + jnp.sum(p, axis=-1, keepdims=True)
        m_ref[...] = m_new
        acc_ref[...] = (acc_ref[...] * jnp.concatenate([alpha] * (KV_LORA // LANES), axis=1)
                        + _dot(p.astype(BF16), values))

    for c in range(n_pages * page // chunk):
        cs = slice(c * chunk, (c + 1) * chunk)
        kc = ckv_buf[slot, cs, :].astype(BF16)
        kp = kpe_buf[slot, :, cs].astype(BF16)
        update(_dot_nt(ql, kc) + _dot(qp, kp), kc)

    t_new = cn_ref.shape[1]
    pad = jnp.zeros((LANES - t_new, KV_LORA), F32)
    kc = jnp.concatenate([cn_ref[0], pad], axis=0).astype(BF16)
    kp = jnp.concatenate([pn_ref[0], pad[:, :LANES]], axis=0).astype(BF16)
    s = _dot_nt(ql, kc) + _dot_nt(q[:, KV_LORA:], kp)
    r = lax.broadcasted_iota(jnp.int32, (rows, LANES), 0)
    c = lax.broadcasted_iota(jnp.int32, (rows, LANES), 1)
    s = jnp.where(c <= r // MLA_HEADS, s, NEG)
    update(s, kc)
    o_ref[0] = (acc_ref[...] / jnp.concatenate([l_ref[...]] * (KV_LORA // LANES), axis=1)).astype(o_ref.dtype)


def _decode_attend(q_abs, ckv_new, kpe_new, cache_ckv, cache_kpe_t, page_table):
    batch, rows, _ = q_abs.shape
    n_pages = page_table.shape[1]
    page = cache_ckv.shape[1]
    past = n_pages * page
    t_new = ckv_new.shape[1]
    per_b = lambda b, pt: (b, 0, 0)
    grid_spec = pltpu.PrefetchScalarGridSpec(
        num_scalar_prefetch=1,
        grid=(batch,),
        in_specs=[pl.BlockSpec((1, rows, LAT_W), per_b),
                  pl.BlockSpec((1, t_new, KV_LORA), per_b),
                  pl.BlockSpec((1, t_new, LANES), per_b),
                  pl.BlockSpec(memory_space=pl.ANY),
                  pl.BlockSpec(memory_space=pl.ANY)],
        out_specs=pl.BlockSpec((1, rows, KV_LORA), per_b),
        scratch_shapes=[pltpu.VMEM((2, past, KV_LORA), F32),
                        pltpu.VMEM((2, QK_ROPE, past), F32),
                        pltpu.SemaphoreType.DMA((2, 2)),
                        pltpu.VMEM((rows, LANES), F32), pltpu.VMEM((rows, LANES), F32),
                        pltpu.VMEM((rows, KV_LORA), F32)],
    )
    return pl.pallas_call(
        functools.partial(_decode_kernel, n_pages=n_pages, page=page, chunk=min(DECODE_CHUNK, past)),
        out_shape=jax.ShapeDtypeStruct((batch, rows, KV_LORA), BF16),
        grid_spec=grid_spec,
        compiler_params=_cparams(("arbitrary",)),
        name="mla_decode",
    )(page_table.reshape(-1), q_abs, ckv_new, kpe_new, cache_ckv, cache_kpe_t)


def _pad_lanes(v, width):
    return jnp.pad(v, [(0, 0)] * (v.ndim - 1) + [(0, width - v.shape[-1])])


def _ssd_params(i, w_in_a, conv_w, conv_b, dt_bias, a_log, d_skip, g_ssd_norm):
    w = w_in_a[i]
    z0, x0, d0, m0 = 0, D_INNER, D_INNER + CONV_DIM, D_INNER + CONV_DIM + SSD_HEADS
    w_main = jnp.concatenate([w[:, x0:d0], w[:, m0:], w[:, z0:x0]], axis=1).astype(BF16)
    w_dt = _pad_lanes(w[:, d0:m0], LANES).astype(BF16)
    head_of = jnp.arange(D_INNER, dtype=jnp.int32) // SSD_HEAD_DIM
    expand = (jnp.arange(LANES, dtype=jnp.int32)[:, None] == head_of[None, :]).astype(BF16)
    return dict(w_main=w_main, w_dt=w_dt, conv_w=conv_w[i], conv_b=conv_b[i].reshape(1, CONV_DIM),
                dt_bias=_pad_lanes(dt_bias[i].reshape(1, SSD_HEADS), LANES),
                a_log=_pad_lanes(a_log[i].reshape(1, SSD_HEADS), LANES),
                d_x=jnp.repeat(d_skip[i], SSD_HEAD_DIM).reshape(1, D_INNER),
                g_norm=g_ssd_norm[i].reshape(1, D_INNER), expand=expand)


def _mla_shared_params(w_kv_a, w_uk, w_uv):
    half = QK_ROPE // 2
    wa = _pad_lanes(w_kv_a, LAT_W).astype(BF16)
    pe = w_kv_a[:, KV_LORA:]
    wr = _pad_lanes(jnp.concatenate([-pe[:, half:], pe[:, :half]], axis=1), LANES).astype(BF16)
    eye = jnp.eye(QK_ROPE, dtype=F32)
    k_blocks, abs_blocks = [], []
    for h in range(MLA_HEADS):
        kb = jnp.zeros((LAT_W, LANES), F32)
        kb = kb.at[:KV_LORA, :QK_NOPE].set(w_uk[:, h, :])
        kb = kb.at[KV_LORA:KV_LORA + QK_ROPE, QK_NOPE:QK_NOPE + QK_ROPE].set(eye)
        k_blocks.append(kb)
        ab = jnp.zeros((LANES, LAT_W), F32)
        ab = ab.at[:QK_NOPE, :KV_LORA].set(w_uk[:, h, :].T)
        ab = ab.at[QK_NOPE:QK_NOPE + QK_ROPE, KV_LORA:KV_LORA + QK_ROPE].set(eye)
        abs_blocks.append(ab)
    w_k = jnp.concatenate(k_blocks, axis=1).astype(BF16)
    w_v = jnp.pad(w_uv.reshape(KV_LORA, MLA_HEADS * V_HEAD), ((0, LAT_W - KV_LORA), (0, 0))).astype(BF16)
    w_abs = jnp.stack(abs_blocks).astype(BF16)
    zero = jnp.zeros((KV_LORA, V_HEAD), F32)
    w_uvp = jnp.stack([jnp.block([[w_uv[:, 2 * j, :], zero], [zero, w_uv[:, 2 * j + 1, :]]])
                       for j in range(MLA_HEADS // 2)]).astype(BF16)
    return dict(wa=wa, wr=wr, w_k=w_k, w_v=w_v, w_abs=w_abs, w_uvp=w_uvp)


def _mla_layer_params(i, w_in_b, w_q_b):
    half = QK_ROPE // 2
    w = w_in_b[i]
    w_inb = jnp.concatenate([w[:, Q_LORA:], w[:, :Q_LORA]], axis=1).astype(BF16)
    wq = w_q_b[i]
    wqa = _pad_lanes(wq, LANES).reshape(Q_LORA, MLA_HEADS * LANES).astype(BF16)
    rot = jnp.concatenate([jnp.zeros_like(wq[..., :QK_NOPE]), -wq[..., QK_NOPE + half:], wq[..., QK_NOPE:QK_NOPE + half]],
                          axis=-1)
    wqb = _pad_lanes(rot, LANES).reshape(Q_LORA, MLA_HEADS * LANES).astype(BF16)
    return dict(w_inb=w_inb, wqa=wqa, wqb=wqb)


def _rope_tables(pos):
    half = QK_ROPE // 2
    inv = ROPE_BASE ** (-jnp.arange(half, dtype=F32) / half)
    ang = pos.astype(F32)[:, None] * inv[None, :]
    cos, sin = jnp.cos(ang), jnp.sin(ang)
    n = pos.shape[0]
    one = lambda w: jnp.ones((n, w), F32)
    zero = lambda w: jnp.zeros((n, w), F32)
    tail = LANES - QK_NOPE - QK_ROPE
    cos_q = jnp.concatenate([one(QK_NOPE), cos, cos, one(tail)], axis=1)
    sin_q = jnp.concatenate([zero(QK_NOPE), sin, sin, zero(tail)], axis=1)
    cos_k = jnp.concatenate([cos, cos, one(LANES - QK_ROPE)], axis=1)
    sin_k = jnp.concatenate([sin, sin, zero(LANES - QK_ROPE)], axis=1)
    return cos_q, sin_q, cos_k, sin_k


def _trunk(x, batch, t, sample, mem_attend, ssd_state, mla_cache, tabs, P):
    act = F32 if sample else BF16
    cos_q, sin_q, cos_k, sin_k = tabs
    ssm_out, conv_out, stacked = [], [], None
    ckv = kpe = None
    for l in range(DEPTH):
        if l < N_A:
            sp = P["ssd"][l]
            proj = _proj(x, sp["w_main"], g=P["g_pre_mix"][l], out_dtype=act, tn=1024, row_tiles=2, name=f"in_a{l}")
            dt = _proj(x, sp["w_dt"], g=P["g_pre_mix"][l], out_dtype=F32, name=f"in_dt{l}")
            if sample:
                state_ssm, state_conv = ssd_state
                y, stacked = _ssd_sample(proj, dt, state_conv, state_ssm.reshape(N_A, batch, D_INNER, D_STATE),
                                         stacked, l, sp, batch, t)
            else:
                y, h_new = _ssd_prompt(proj, dt, sp, batch, t)
                ssm_out.append(h_new)
                conv_out.append(proj.reshape(batch, t, -1)[:, t - 3:, :CONV_DIM].astype(F32))
            mo = mem_attend(l, proj, 3)
            w_out = P["w_out_a"][l]
            x = _outproj(y, mo, w_out[:D_INNER], w_out[D_INNER:], P["g_post_mix"][l], x, f"out_a{l}")
        else:
            i = l - N_A
            ms, ml = P["mla_shared"], P["mla"][i]
            if i == 0:
                ckv, kpe, lat = _kv_latent(x, P["g_kv_in"], ms["wa"], ms["wr"], P["g_kv_norm"], cos_k, sin_k)
                if not sample:
                    k_all = _proj(lat, ms["w_k"], tn=1024, name="k_heads")
                    v_all = _proj(lat, ms["w_v"], name="v_heads")
            proj = _proj(x, ml["w_inb"], g=P["g_pre_mix"][l], out_dtype=act, name=f"in_b{i}")
            q2 = _proj(proj, ml["wqa"], g=P["g_q_norm"][i], xcol=2, tn=1024,
                       rope=(ml["wqb"], cos_q, sin_q, MLA_SCALE), name=f"q_heads{i}")
            if sample:
                cache_ckv, cache_kpe, page_table = mla_cache
                q_abs = _block_proj(q2, ms["w_abs"], f"q_abs{i}").reshape(batch, t * MLA_HEADS, LAT_W)
                o_lat = _decode_attend(q_abs, ckv.reshape(batch, t, KV_LORA), kpe.reshape(batch, t, LANES),
                                       cache_ckv, cache_kpe, page_table)
                o = _block_proj(o_lat.reshape(batch * t, MLA_HEADS * KV_LORA), ms["w_uvp"], f"o_heads{i}")
            else:
                o = _flash_attend(q2, k_all, v_all, batch, t)
            mo = mem_attend(l, proj, 0)
            w_out = P["w_out_b"][i]
            n_o = MLA_HEADS * V_HEAD
            x = _outproj(o, mo, w_out[:n_o], w_out[n_o:], P["g_post_mix"][l], x, f"out_b{i}")
        x = _ffn(x, P["g_pre_ffn"][l], P["w_ffn_up"][l], P["w_ffn_down"][l], P["g_post_ffn"][l], f"ffn{l}")
    conv_new, ssm_new = stacked if sample else (jnp.stack(conv_out), jnp.stack(ssm_out))
    ssm_new = ssm_new.reshape(N_A, batch, SSD_HEADS, SSD_HEAD_DIM, D_STATE)
    return x, ssm_new, conv_new, ckv, kpe[:, :QK_ROPE]


def kernel(x_prompt, x_sample, mem_prompt, state_ssm, state_conv, cache_ckv, cache_kpe, cache_mem_k, cache_mem_v, page_table, g_pre_mix, g_post_mix, g_pre_ffn, g_post_ffn, w_ffn_up, w_ffn_down, w_mem_k, w_mem_v, w_in_a, conv_w, conv_b, dt_bias, a_log, d_skip, g_ssd_norm, w_out_a, g_kv_in, w_kv_a, g_kv_norm, w_uk, w_uv, w_in_b, g_q_norm, w_q_b, w_out_b):
    b, s, _ = x_prompt.shape
    db, t, _ = x_sample.shape
    past_len = page_table.shape[1] * cache_ckv.shape[1]

    P = dict(
        g_pre_mix=g_pre_mix, g_post_mix=g_post_mix, g_pre_ffn=g_pre_ffn, g_post_ffn=g_post_ffn,
        g_kv_in=g_kv_in, g_kv_norm=g_kv_norm, g_q_norm=g_q_norm,
        w_ffn_up=w_ffn_up.astype(BF16), w_ffn_down=w_ffn_down.astype(BF16),
        w_out_a=w_out_a.astype(BF16), w_out_b=w_out_b.astype(BF16),
        ssd=[_ssd_params(i, w_in_a, conv_w, conv_b, dt_bias, a_log, d_skip, g_ssd_norm) for i in range(N_A)],
        mla_shared=_mla_shared_params(w_kv_a, w_uk, w_uv),
        mla=[_mla_layer_params(i, w_in_b, w_q_b) for i in range(DEPTH - N_A)],
    )

    w_mem = jnp.concatenate([w_mem_k[l] for l in range(DEPTH)] + [w_mem_v[l] for l in range(DEPTH)], axis=1)
    mem_kv = _proj(mem_prompt.reshape(b * N_MEM, D_MODEL), w_mem.astype(BF16), out_dtype=F32, tn=1024, name="mem_kv")
    mem_kv5 = mem_kv.reshape(b, N_MEM, 2 * DEPTH, MEM_HEADS, MEM_HEAD_DIM)
    p_mem_k = jnp.transpose(mem_kv5[:, :, :DEPTH], (2, 0, 1, 3, 4))
    p_mem_v = jnp.transpose(mem_kv5[:, :, DEPTH:], (2, 0, 1, 3, 4))
    mem_kv3 = mem_kv.reshape(b, N_MEM, 2 * DEPTH * MEM_WIDTH)
    tabs_p = _rope_tables(jnp.arange(s, dtype=jnp.int32))
    y_prompt, p_ssm, p_conv, p_ckv, p_kpe = _trunk(
        x_prompt.reshape(b * s, D_MODEL), b, s, False,
        lambda l, q, qcol: _mem_attend(q, qcol, mem_kv3, mem_kv3, 0, l, DEPTH + l, b, s, BF16, f"mem{l}"),
        None, None, tabs_p, P)

    tabs_s = tuple(jnp.tile(tb, (db, 1)) for tb in _rope_tables(past_len + jnp.arange(t, dtype=jnp.int32)))
    cache_kpe_t = jnp.swapaxes(cache_kpe, 1, 2)
    y_sample, s_ssm, s_conv, s_ckv, s_kpe = _trunk(
        x_sample.reshape(db * t, D_MODEL), db, t, True,
        lambda l, q, qcol: _mem_attend_cache(q, qcol, cache_mem_k, cache_mem_v, l, db, t, f"mem_s{l}"),
        (state_ssm, state_conv), (cache_ckv, cache_kpe_t, page_table), tabs_s, P)

    return (y_prompt.reshape(b, s, D_MODEL), y_sample.reshape(db, t, D_MODEL),
            p_ssm, p_conv, p_ckv.reshape(b, s, KV_LORA), p_kpe.reshape(b, s, QK_ROPE), p_mem_k, p_mem_v,
            s_ssm, s_conv, s_ckv.reshape(db, t, KV_LORA), s_kpe.reshape(db, t, QK_ROPE))
```

```python
import functools

import jax
import jax.numpy as jnp
from jax import lax
from jax.experimental import pallas as pl
from jax.experimental.pallas import tpu as pltpu

F32 = jnp.float32
BF16 = jnp.bfloat16

D_MODEL = 1024
DEPTH = 4
N_A = 2
D_INNER = 2048
SSD_HEADS = 32
SSD_HEAD_DIM = 64
SSD_GROUPS = 4
GROUP_W = D_INNER // SSD_GROUPS
D_STATE = 128
CONV_DIM = 3072
SSD_CHUNK = 128
MLA_HEADS = 16
Q_LORA = 512
KV_LORA = 256
QK_NOPE = 64
QK_ROPE = 32
V_HEAD = 64
ROPE_BASE = 10000.0
MLA_SCALE = (QK_NOPE + QK_ROPE) ** -0.5
LOG2E = 1.4426950408889634
N_MEM = 256
MEM_HEADS = 4
MEM_HEAD_DIM = 256
MEM_WIDTH = 1024
D_FF = 4096
RMS_EPS = 1e-6
LANES = 128
LAT_W = 384
NEG = -1e30
ROW_TILE = 512
ATT_TILE = 1024
ATT_SUB = 512
DECODE_CHUNK = 2048
MEM_BATCH_BLOCK = 4

NT_DIMS = (((1,), (1,)), ((), ()))
TN_DIMS = (((0,), (0,)), ((), ()))


def _cparams(sem, vmem_mb=48):
    return pltpu.CompilerParams(dimension_semantics=sem, vmem_limit_bytes=vmem_mb * 1024 * 1024)


def _rms(x, g):
    ms = jnp.mean(x * x, axis=-1, keepdims=True)
    return x * lax.rsqrt(ms + RMS_EPS) * g


def _silu(x):
    return x * (1.0 / (1.0 + jnp.exp(-x)))


def _softplus(x):
    return jnp.maximum(x, 0.0) + jnp.log1p(jnp.exp(-jnp.abs(x)))


def _dot(a, b):
    return jnp.dot(a, b, preferred_element_type=F32)


def _dot_nt(a, b):
    return lax.dot_general(a, b, NT_DIMS, preferred_element_type=F32)


def _dot_tn(a, b):
    return lax.dot_general(a, b, TN_DIMS, preferred_element_type=F32)


def _split_bf16(v, terms):
    parts = []
    for _ in range(terms):
        p = v.astype(BF16)
        parts.append(p)
        v = v - p.astype(F32)
    return parts


def _sel_dot(m, v, terms=2):
    return sum(_dot(m, p) for p in _split_bf16(v, terms))


def _expand(v, e_ref, terms=2):
    return sum(_dot(p, e_ref[...]) for p in _split_bf16(v, terms))


def _proj_kernel(*refs, has_g, has_rope, scale):
    it = iter(refs)
    x_ref = next(it)
    g_ref = next(it) if has_g else None
    w_ref = next(it)
    if has_rope:
        w2_ref, cos_ref, sin_ref = next(it), next(it), next(it)
    o_ref = next(it)
    xn_ref = next(it)

    @pl.when(pl.program_id(1) == 0)
    def _():
        x = x_ref[...].astype(F32)
        if has_g:
            x = _rms(x, g_ref[...])
        xn_ref[...] = x.astype(BF16)

    xn = xn_ref[...]
    y = _dot(xn, w_ref[...])
    if has_rope:
        rep = y.shape[1] // LANES
        cos = jnp.concatenate([cos_ref[...]] * rep, axis=1)
        sin = jnp.concatenate([sin_ref[...]] * rep, axis=1)
        y = (y * cos + _dot(xn, w2_ref[...]) * sin) * scale
    o_ref[...] = y.astype(o_ref.dtype)


def _proj(x, w, *, g=None, xcol=0, out_dtype=BF16, tn=None, rope=None, row_tiles=1, name):
    n = x.shape[0]
    k, nout = w.shape
    tm = min(ROW_TILE * row_tiles, n)
    tn = nout if tn is None else tn
    in_specs = [pl.BlockSpec((tm, k), lambda i, j: (i, xcol))]
    args = [x]
    if g is not None:
        in_specs.append(pl.BlockSpec((1, k), lambda i, j: (0, 0)))
        args.append(g.reshape(1, k).astype(F32))
    in_specs.append(pl.BlockSpec((k, tn), lambda i, j: (0, j)))
    args.append(w)
    scale = 1.0
    if rope is not None:
        w2, cos, sin, scale = rope
        nb = cos.shape[0] // tm
        in_specs.append(pl.BlockSpec((k, tn), lambda i, j: (0, j)))
        in_specs.append(pl.BlockSpec((tm, LANES), lambda i, j: (i % nb, 0)))
        in_specs.append(pl.BlockSpec((tm, LANES), lambda i, j: (i % nb, 0)))
        args += [w2, cos, sin]
    return pl.pallas_call(
        functools.partial(_proj_kernel, has_g=g is not None, has_rope=rope is not None, scale=scale),
        out_shape=jax.ShapeDtypeStruct((n, nout), out_dtype),
        grid=(n // tm, nout // tn),
        in_specs=in_specs,
        out_specs=pl.BlockSpec((tm, tn), lambda i, j: (i, j)),
        scratch_shapes=[pltpu.VMEM((tm, k), BF16)],
        compiler_params=_cparams(("parallel", "arbitrary")),
        name=name,
    )(*args)


def _proj_t_kernel(x_ref, w_ref, o_ref):
    o_ref[...] = _dot_nt(w_ref[...], x_ref[...]).astype(o_ref.dtype)


def _proj_t(x, w_t, name):
    n, k = x.shape
    nout = w_t.shape[0]
    tm = min(ROW_TILE, n)
    return pl.pallas_call(
        _proj_t_kernel,
        out_shape=jax.ShapeDtypeStruct((nout, n), BF16),
        grid=(n // tm,),
        in_specs=[pl.BlockSpec((tm, k), lambda i: (i, 0)), pl.BlockSpec((nout, k), lambda i: (0, 0))],
        out_specs=pl.BlockSpec((nout, tm), lambda i: (0, i)),
        compiler_params=_cparams(("parallel",)),
        name=name,
    )(x, w_t)


def _block_proj_kernel(x_ref, w_ref, o_ref):
    o_ref[...] = _dot(x_ref[...].astype(BF16), w_ref[0]).astype(o_ref.dtype)


def _block_proj(x, w, name):
    n = x.shape[0]
    nb, kx, ko = w.shape
    tm = min(ROW_TILE, n)
    return pl.pallas_call(
        _block_proj_kernel,
        out_shape=jax.ShapeDtypeStruct((n, nb * ko), BF16),
        grid=(n // tm, nb),
        in_specs=[pl.BlockSpec((tm, kx), lambda i, j: (i, j)), pl.BlockSpec((1, kx, ko), lambda i, j: (j, 0, 0))],
        out_specs=pl.BlockSpec((tm, ko), lambda i, j: (i, j)),
        compiler_params=_cparams(("parallel", "parallel")),
        name=name,
    )(x, w)


def _kvlat_kernel(x_ref, g_ref, wa_ref, wr_ref, gkv_ref, cos_ref, sin_ref, ckv_ref, kpe_ref, lat_ref):
    xn = _rms(x_ref[...], g_ref[...]).astype(BF16)
    a = _dot(xn, wa_ref[...])
    r = _dot(xn, wr_ref[...])
    ckv = _rms(a[:, :KV_LORA], gkv_ref[...])
    pe = a[:, KV_LORA:] * cos_ref[...] + r * sin_ref[...]
    ckv_ref[...] = ckv
    kpe_ref[...] = pe
    lat_ref[:, :KV_LORA] = ckv.astype(BF16)
    lane = lax.broadcasted_iota(jnp.int32, pe.shape, 1)
    lat_ref[:, KV_LORA:] = jnp.where(lane == QK_ROPE, 1.0, pe).astype(BF16)


def _kv_latent(x, g, wa, wr, gkv, cos, sin):
    n = x.shape[0]
    tm = min(ROW_TILE, n)
    nb = cos.shape[0] // tm
    row = lambda i: (i, 0)
    const = lambda i: (0, 0)
    return pl.pallas_call(
        _kvlat_kernel,
        out_shape=(jax.ShapeDtypeStruct((n, KV_LORA), F32),
                   jax.ShapeDtypeStruct((n, LANES), F32),
                   jax.ShapeDtypeStruct((n, LAT_W), BF16)),
        grid=(n // tm,),
        in_specs=[pl.BlockSpec((tm, D_MODEL), row),
                  pl.BlockSpec((1, D_MODEL), const),
                  pl.BlockSpec((D_MODEL, LAT_W), const),
                  pl.BlockSpec((D_MODEL, LANES), const),
                  pl.BlockSpec((1, KV_LORA), const),
                  pl.BlockSpec((tm, LANES), lambda i: (i % nb, 0)),
                  pl.BlockSpec((tm, LANES), lambda i: (i % nb, 0))],
        out_specs=(pl.BlockSpec((tm, KV_LORA), row),
                   pl.BlockSpec((tm, LANES), row),
                   pl.BlockSpec((tm, LAT_W), row)),
        compiler_params=_cparams(("parallel",)),
        name="kv_latent",
    )(x, g.reshape(1, D_MODEL), wa, wr, gkv.reshape(1, KV_LORA), cos, sin)


def _outproj_kernel(a1_ref, a2_ref, w1_ref, w2_ref, g_ref, r_ref, o_ref):
    acc = _dot(a1_ref[...].astype(BF16), w1_ref[...]) + _dot(a2_ref[...].astype(BF16), w2_ref[...])
    o_ref[...] = r_ref[...] + _rms(acc, g_ref[...])


def _outproj(a1, a2, w1, w2, g, res, name):
    n = res.shape[0]
    tm = min(ROW_TILE, n)
    k1, k2 = w1.shape[0], w2.shape[0]
    row = lambda i: (i, 0)
    const = lambda i: (0, 0)
    return pl.pallas_call(
        _outproj_kernel,
        out_shape=jax.ShapeDtypeStruct((n, D_MODEL), F32),
        grid=(n // tm,),
        in_specs=[pl.BlockSpec((tm, k1), row), pl.BlockSpec((tm, k2), row),
                  pl.BlockSpec((k1, D_MODEL), const), pl.BlockSpec((k2, D_MODEL), const),
                  pl.BlockSpec((1, D_MODEL), const), pl.BlockSpec((tm, D_MODEL), row)],
        out_specs=pl.BlockSpec((tm, D_MODEL), row),
        compiler_params=_cparams(("parallel",)),
        name=name,
    )(a1, a2, w1, w2, g.reshape(1, D_MODEL), res)


def _ffn_kernel(x_ref, g1_ref, wu_ref, wd_ref, g2_ref, o_ref):
    x = x_ref[...]
    xn = _rms(x, g1_ref[...]).astype(BF16)
    acc = jnp.zeros(x.shape, F32)
    for c in range(D_FF // D_MODEL):
        sl = slice(c * D_MODEL, (c + 1) * D_MODEL)
        h = _dot(xn, wu_ref[:, sl])
        h = jnp.square(jnp.maximum(h, 0.0)).astype(BF16)
        acc = acc + _dot(h, wd_ref[sl, :])
    o_ref[...] = x + _rms(acc, g2_ref[...])


def _ffn(x, g1, wu, wd, g2, name):
    n = x.shape[0]
    tm = min(ROW_TILE, n)
    row = lambda i: (i, 0)
    const = lambda i: (0, 0)
    return pl.pallas_call(
        _ffn_kernel,
        out_shape=jax.ShapeDtypeStruct((n, D_MODEL), F32),
        grid=(n // tm,),
        in_specs=[pl.BlockSpec((tm, D_MODEL), row), pl.BlockSpec((1, D_MODEL), const),
                  pl.BlockSpec((D_MODEL, D_FF), const, pipeline_mode=pl.Buffered(1)),
                  pl.BlockSpec((D_FF, D_MODEL), const, pipeline_mode=pl.Buffered(1)),
                  pl.BlockSpec((1, D_MODEL), const)],
        out_specs=pl.BlockSpec((tm, D_MODEL), row),
        compiler_params=_cparams(("parallel",), vmem_mb=48),
        name=name,
    )(x, g1.reshape(1, D_MODEL), wu, wd, g2.reshape(1, D_MODEL))


def _mem_head(qh, kh, vh):
    s = _dot_nt(qh.astype(BF16), kh.astype(BF16)) * (MEM_HEAD_DIM ** -0.5)
    m = jnp.max(s, axis=-1, keepdims=True)
    p = jnp.exp(s - m)
    l = jnp.sum(p, axis=-1, keepdims=True)
    return _dot(p.astype(BF16), vh.astype(BF16)) / l


def _mem_kernel(q_ref, k_ref, v_ref, o_ref):
    q = q_ref[...]
    for h in range(MEM_HEADS):
        sl = slice(h * MEM_HEAD_DIM, (h + 1) * MEM_HEAD_DIM)
        o_ref[:, sl] = _mem_head(q[:, sl], k_ref[0, :, sl], v_ref[0, :, sl]).astype(o_ref.dtype)


def _mem_cache_kernel(q_ref, k_hbm, v_hbm, o_ref, kbuf, vbuf, sem, *, layer, bb, t):
    step = pl.program_id(0)
    slot = step % 2

    def copies(st, sl):
        out = []
        for i in range(bb):
            for h in range(MEM_HEADS):
                out.append(pltpu.make_async_copy(k_hbm.at[layer, st * bb + i, :, h, :], kbuf.at[sl, i, h], sem.at[0, sl]))
                out.append(pltpu.make_async_copy(v_hbm.at[layer, st * bb + i, :, h, :], vbuf.at[sl, i, h], sem.at[1, sl]))
        return out

    @pl.when(step == 0)
    def _():
        for cp in copies(0, 0):
            cp.start()

    @pl.when(step + 1 < pl.num_programs(0))
    def _():
        for cp in copies(step + 1, 1 - slot):
            cp.start()

    for cp in copies(step, slot):
        cp.wait()

    r = lax.broadcasted_iota(jnp.int32, (MEM_HEADS * t, MEM_HEADS * N_MEM), 0)
    c = lax.broadcasted_iota(jnp.int32, (MEM_HEADS * t, MEM_HEADS * N_MEM), 1)
    same_head = (r // t) == (c // N_MEM)
    for i in range(bb):
        rows = slice(i * t, (i + 1) * t)
        qs = jnp.concatenate([q_ref[rows, h * MEM_HEAD_DIM:(h + 1) * MEM_HEAD_DIM] for h in range(MEM_HEADS)], axis=0)
        ks = jnp.concatenate([kbuf[slot, i, h].astype(BF16) for h in range(MEM_HEADS)], axis=0)
        vs = jnp.concatenate([vbuf[slot, i, h].astype(BF16) for h in range(MEM_HEADS)], axis=0)
        s = jnp.where(same_head, _dot_nt(qs.astype(BF16), ks) * (MEM_HEAD_DIM ** -0.5), NEG)
        p = jnp.exp(s - jnp.max(s, axis=-1, keepdims=True))
        o = _dot(p.astype(BF16), vs) / jnp.sum(p, axis=-1, keepdims=True)
        for h in range(MEM_HEADS):
            o_ref[rows, h * MEM_HEAD_DIM:(h + 1) * MEM_HEAD_DIM] = o[h * t:(h + 1) * t, :].astype(o_ref.dtype)


def _mem_attend_cache(q, qcol, cache_k, cache_v, layer, batch, t, name):
    bb = min(MEM_BATCH_BLOCK, batch)
    buf = pltpu.VMEM((2, bb, MEM_HEADS, N_MEM, MEM_HEAD_DIM), F32)
    return pl.pallas_call(
        functools.partial(_mem_cache_kernel, layer=layer, bb=bb, t=t),
        out_shape=jax.ShapeDtypeStruct((batch * t, MEM_WIDTH), F32),
        grid=(batch // bb,),
        in_specs=[pl.BlockSpec((bb * t, MEM_WIDTH), lambda i: (i, qcol)),
                  pl.BlockSpec(memory_space=pl.ANY), pl.BlockSpec(memory_space=pl.ANY)],
        out_specs=pl.BlockSpec((bb * t, MEM_WIDTH), lambda i: (i, 0)),
        scratch_shapes=[buf, buf, pltpu.SemaphoreType.DMA((2, 2))],
        compiler_params=_cparams(("arbitrary",)),
        name=name,
    )(q, cache_k, cache_v)


def _mem_attend(q, qcol, k_arr, v_arr, kboff, kcol, vcol, batch, t, out_dtype, name):
    tq = min(ROW_TILE, t)
    nt = t // tq
    return pl.pallas_call(
        _mem_kernel,
        out_shape=jax.ShapeDtypeStruct((batch * t, MEM_WIDTH), out_dtype),
        grid=(batch, nt),
        in_specs=[pl.BlockSpec((tq, MEM_WIDTH), lambda b, i: (b * nt + i, qcol)),
                  pl.BlockSpec((1, N_MEM, MEM_WIDTH), lambda b, i: (b + kboff, 0, kcol)),
                  pl.BlockSpec((1, N_MEM, MEM_WIDTH), lambda b, i: (b + kboff, 0, vcol))],
        out_specs=pl.BlockSpec((tq, MEM_WIDTH), lambda b, i: (b * nt + i, 0)),
        compiler_params=_cparams(("parallel", "parallel")),
        name=name,
    )(q, k_arr, v_arr)


def _gate_norm_store(y_ref, ygrp, z_ref, gn_ref, g):
    sl = slice(g * GROUP_W, (g + 1) * GROUP_W)
    ygrp = ygrp * _silu(z_ref[:, sl].astype(F32))
    y_ref[:, sl] = _rms(ygrp, gn_ref[:, sl]).astype(y_ref.dtype)


def _ssd_prompt_kernel(xs_ref, bc_ref, z_ref, dt_ref, cw_ref, cb_ref, dtb_ref, alog_ref, d_ref, gn_ref, e_ref,
                       y_ref, hout_ref, extx, extbc, ht):
    q = SSD_CHUNK
    c = pl.program_id(1)

    @pl.when(c == 0)
    def _():
        extx[0:8] = jnp.zeros((8, D_INNER), F32)
        extbc[0:8] = jnp.zeros((8, CONV_DIM - D_INNER), F32)
        ht[...] = jnp.zeros(ht.shape, F32)

    @pl.when(c > 0)
    def _():
        extx[0:8] = extx[q:q + 8]
        extbc[0:8] = extbc[q:q + 8]

    extx[8:q + 8] = xs_ref[...].astype(F32)
    extbc[8:q + 8] = bc_ref[...].astype(F32)

    def conv(ext, w, b):
        acc = b + w[0:1] * ext[5:q + 5] + w[1:2] * ext[6:q + 6] + w[2:3] * ext[7:q + 7] + w[3:4] * ext[8:q + 8]
        return _silu(acc)

    cw = cw_ref[...]
    cb = cb_ref[...]
    xs = conv(extx, cw[:, :D_INNER], cb[:, :D_INNER])
    bcv = conv(extbc, cw[:, D_INNER:], cb[:, D_INNER:])

    dt = _softplus(dt_ref[...] + dtb_ref[...])
    a = dt * (-jnp.exp(alog_ref[...]))
    row = lax.broadcasted_iota(jnp.int32, (q, q), 0)
    col = lax.broadcasted_iota(jnp.int32, (q, q), 1)
    causal = row >= col
    acs = _sel_dot(jnp.where(causal, 1.0, 0.0).astype(BF16), a, terms=3)
    acs_t = acs.T
    a_last = acs[q - 1:q, :]
    dtx = _expand(dt, e_ref)
    dwx = _expand(dt * jnp.exp(a_last - acs), e_ref)
    eax = _expand(jnp.exp(acs), e_ref)
    xdt = xs * dtx
    xw_b = (xs * dwx).astype(BF16)
    lane = lax.broadcasted_iota(jnp.int32, (q, LANES), 1)
    heads_per_group = SSD_HEADS // SSD_GROUPS

    for g in range(SSD_GROUPS):
        gs = slice(g * GROUP_W, (g + 1) * GROUP_W)
        bg = bcv[:, g * D_STATE:(g + 1) * D_STATE].astype(BF16)
        cg = bcv[:, (SSD_GROUPS + g) * D_STATE:(SSD_GROUPS + g + 1) * D_STATE].astype(BF16)
        cbm = _dot_nt(cg, bg)
        htg = ht[:, gs]
        y_off = _dot(cg, htg.astype(BF16)) * eax[:, gs]
        ht[:, gs] = eax[q - 1:q, gs] * htg + _dot_tn(bg, xw_b[:, gs])
        pairs = []
        for j in range(heads_per_group // 2):
            h0 = g * heads_per_group + 2 * j
            xpair = xdt[:, h0 * SSD_HEAD_DIM:(h0 + 2) * SSD_HEAD_DIM]
            yp = None
            for k in range(2):
                h = h0 + k
                seg = acs[:, h:h + 1] - acs_t[h:h + 1, :]
                m = (cbm * jnp.exp(jnp.where(causal, seg, NEG))).astype(BF16)
                keep = (lane < SSD_HEAD_DIM) if k == 0 else (lane >= SSD_HEAD_DIM)
                t = _dot(m, jnp.where(keep, xpair, 0.0).astype(BF16))
                yp = t if yp is None else yp + t
            pairs.append(yp)
        ygrp = jnp.concatenate(pairs, axis=1) + y_off + d_ref[:, gs] * xs[:, gs]
        _gate_norm_store(y_ref, ygrp, z_ref, gn_ref, g)

    @pl.when(c == pl.num_programs(1) - 1)
    def _():
        hout_ref[0] = ht[...].T


def _ssd_prompt(proj, dt, sp, batch, t):
    nc = t // SSD_CHUNK
    q = SSD_CHUNK
    const = lambda b, c: (0, 0)
    return pl.pallas_call(
        _ssd_prompt_kernel,
        out_shape=(jax.ShapeDtypeStruct((batch * t, D_INNER), BF16),
                   jax.ShapeDtypeStruct((batch, D_INNER, D_STATE), F32)),
        grid=(batch, nc),
        in_specs=[pl.BlockSpec((q, D_INNER), lambda b, c: (b * nc + c, 0)),
                  pl.BlockSpec((q, 1024), lambda b, c: (b * nc + c, 2)),
                  pl.BlockSpec((q, D_INNER), lambda b, c: (b * nc + c, 2)),
                  pl.BlockSpec((q, LANES), lambda b, c: (b * nc + c, 0)),
                  pl.BlockSpec((4, CONV_DIM), const), pl.BlockSpec((1, CONV_DIM), const),
                  pl.BlockSpec((1, LANES), const), pl.BlockSpec((1, LANES), const),
                  pl.BlockSpec((1, D_INNER), const), pl.BlockSpec((1, D_INNER), const),
                  pl.BlockSpec((LANES, D_INNER), const)],
        out_specs=(pl.BlockSpec((q, D_INNER), lambda b, c: (b * nc + c, 0)),
                   pl.BlockSpec((1, D_INNER, D_STATE), lambda b, c: (b, 0, 0))),
        scratch_shapes=[pltpu.VMEM((q + 8, D_INNER), F32),
                        pltpu.VMEM((q + 8, CONV_DIM - D_INNER), F32),
                        pltpu.VMEM((D_STATE, D_INNER), F32)],
        compiler_params=_cparams(("parallel", "arbitrary"), vmem_mb=48),
        name="ssd_prompt",
    )(proj, proj, proj, dt, sp["conv_w"], sp["conv_b"], sp["dt_bias"], sp["a_log"], sp["d_x"], sp["g_norm"], sp["expand"])


def _ssd_sample_kernel(xs_ref, bc_ref, z_ref, dt_ref, cs_ref, h0_ref, cw_ref, cb_ref, dtb_ref, alog_ref, d_ref,
                       gn_ref, e_ref, *rest, n_carried, out_layer):
    y_ref, cnew_ref, hnew_ref, ext = rest[n_carried:]
    t = xs_ref.shape[0]
    for other in range(cnew_ref.shape[0]):
        if other != out_layer:
            cnew_ref[other] = jnp.zeros(cnew_ref.shape[1:], F32)
            hnew_ref[other] = jnp.zeros(hnew_ref.shape[1:], F32)
    ext[5:8] = cs_ref[0]
    ext[8:8 + t, 0:D_INNER] = xs_ref[...]
    ext[8:8 + t, D_INNER:CONV_DIM] = bc_ref[...]
    cw = cw_ref[...]
    acc = (cb_ref[...] + cw[0:1] * ext[5:5 + t] + cw[1:2] * ext[6:6 + t]
           + cw[2:3] * ext[7:7 + t] + cw[3:4] * ext[8:8 + t])
    cnew_ref[out_layer, 0] = ext[5 + t:8 + t]
    xbc = _silu(acc)
    xs = xbc[:, :D_INNER]

    dt = _softplus(dt_ref[...] + dtb_ref[...])
    a = dt * (-jnp.exp(alog_ref[...]))
    rowi = lax.broadcasted_iota(jnp.int32, (t, LANES), 0)
    acs = jnp.zeros((t, LANES), F32)
    for s in range(t):
        acs = acs + jnp.where(rowi >= s, a[s:s + 1, :], 0.0)
    a_last = acs[t - 1:t, :]
    eacs = jnp.exp(acs)
    acsx = _expand(acs, e_ref, terms=3)
    dtx = _expand(dt, e_ref)
    dwx = _expand(dt * jnp.exp(a_last - acs), e_ref)
    eax = _expand(eacs, e_ref)
    xdt = xs * dtx
    xw_b = (xs * dwx).astype(BF16)

    def grp(base, g):
        return xbc[:, D_INNER + (base + g) * D_STATE:D_INNER + (base + g + 1) * D_STATE]

    li = lax.broadcasted_iota(jnp.int32, (t, D_INNER), 0)
    y = jnp.zeros((t, D_INNER), F32)
    for s in range(t):
        decay = jnp.exp(jnp.where(li >= s, acsx - acsx[s:s + 1, :], NEG))
        cbx = jnp.concatenate(
            [jnp.broadcast_to(jnp.sum(grp(SSD_GROUPS, g) * grp(0, g)[s:s + 1, :], axis=-1, keepdims=True),
                              (t, GROUP_W)) for g in range(SSD_GROUPS)], axis=1)
        y = y + cbx * decay * xdt[s:s + 1, :]

    last_only = jnp.where(rowi == t - 1, eacs, 0.0)
    ones = jnp.ones((t, LANES), BF16)
    rdec = sum(_dot_tn(p, ones) for p in _split_bf16(last_only, 3))

    heads_per_group = SSD_HEADS // SSD_GROUPS
    for g in range(SSD_GROUPS):
        gs = slice(g * GROUP_W, (g + 1) * GROUP_W)
        bg = grp(0, g).astype(BF16)
        cg = grp(SSD_GROUPS, g).astype(BF16)
        h0g = h0_ref[0, gs, :]
        y_off = _dot_nt(cg, h0g.astype(BF16)) * eax[:, gs]
        upd = _dot_tn(xw_b[:, gs], bg)
        for hh in range(heads_per_group):
            h = g * heads_per_group + hh
            rs = slice(hh * SSD_HEAD_DIM, (hh + 1) * SSD_HEAD_DIM)
            hnew_ref[out_layer, 0, h * SSD_HEAD_DIM:(h + 1) * SSD_HEAD_DIM, :] = (
                rdec[h:h + 1, :] * h0g[rs, :] + upd[rs, :])
        ygrp = y[:, gs] + y_off + d_ref[:, gs] * xs[:, gs]
        _gate_norm_store(y_ref, ygrp, z_ref, gn_ref, g)


def _ssd_sample(proj, dt, conv_all, ssm_all, stacked, layer, sp, batch, t):
    const = lambda b: (0, 0)
    n_layers = ssm_all.shape[0]
    conv_spec = pl.BlockSpec((None, 1, 3, CONV_DIM), lambda b: (layer, b, 0, 0))
    ssm_spec = pl.BlockSpec((None, 1, D_INNER, D_STATE), lambda b: (layer, b, 0, 0))
    in_specs = [pl.BlockSpec((t, D_INNER), lambda b: (b, 0)),
                pl.BlockSpec((t, 1024), lambda b: (b, 2)),
                pl.BlockSpec((t, D_INNER), lambda b: (b, 2)),
                pl.BlockSpec((t, LANES), lambda b: (b, 0)),
                conv_spec, ssm_spec,
                pl.BlockSpec((4, CONV_DIM), const), pl.BlockSpec((1, CONV_DIM), const),
                pl.BlockSpec((1, LANES), const), pl.BlockSpec((1, LANES), const),
                pl.BlockSpec((1, D_INNER), const), pl.BlockSpec((1, D_INNER), const),
                pl.BlockSpec((LANES, D_INNER), const)]
    args = [proj, proj, proj, dt, conv_all, ssm_all, sp["conv_w"], sp["conv_b"], sp["dt_bias"], sp["a_log"],
            sp["d_x"], sp["g_norm"], sp["expand"]]
    if stacked is None:
        aliases = {}
        out_layer = layer
        conv_out = pl.BlockSpec((n_layers, 1, 3, CONV_DIM), lambda b: (0, b, 0, 0))
        ssm_out = pl.BlockSpec((n_layers, 1, D_INNER, D_STATE), lambda b: (0, b, 0, 0))
    else:
        aliases = {len(args): 1, len(args) + 1: 2}
        in_specs += [pl.BlockSpec(memory_space=pl.ANY), pl.BlockSpec(memory_space=pl.ANY)]
        args += list(stacked)
        out_layer = 0
        conv_out = pl.BlockSpec((1, 1, 3, CONV_DIM), lambda b: (layer, b, 0, 0))
        ssm_out = pl.BlockSpec((1, 1, D_INNER, D_STATE), lambda b: (layer, b, 0, 0))
    y, conv_new, ssm_new = pl.pallas_call(
        functools.partial(_ssd_sample_kernel, n_carried=len(aliases), out_layer=out_layer),
        out_shape=(jax.ShapeDtypeStruct((batch * t, D_INNER), F32),
                   jax.ShapeDtypeStruct((n_layers, batch, 3, CONV_DIM), F32),
                   jax.ShapeDtypeStruct((n_layers, batch, D_INNER, D_STATE), F32)),
        grid=(batch,),
        in_specs=in_specs,
        out_specs=(pl.BlockSpec((t, D_INNER), lambda b: (b, 0)), conv_out, ssm_out),
        scratch_shapes=[pltpu.VMEM((8 + t, CONV_DIM), F32)],
        input_output_aliases=aliases,
        compiler_params=_cparams(("parallel",)),
        name="ssd_sample",
    )(*args)
    return y, (conv_new, ssm_new)


def _flash_kernel(qt_ref, kt_ref, q_ref, k_ref, v_ref, o_ref, m_ref, acc_ref, *, sub):
    step = pl.program_id(2)
    qi = qt_ref[step]
    ki = kt_ref[step]
    nsub = q_ref.shape[0] // sub

    @pl.when(ki == 0)
    def _():
        m_ref[...] = jnp.full(m_ref.shape, NEG, F32)
        acc_ref[...] = jnp.zeros(acc_ref.shape, F32)

    def unit(qs, ks, masked):
        rows = slice(qs * sub, (qs + 1) * sub)
        cols = slice(ks * sub, (ks + 1) * sub)
        for h in range(2):
            hs = slice(h * LANES, (h + 1) * LANES)
            s = _dot(q_ref[rows, hs], k_ref[hs, cols])
            if masked:
                r = lax.broadcasted_iota(jnp.int32, (sub, sub), 0)
                c = lax.broadcasted_iota(jnp.int32, (sub, sub), 1)
                s = jnp.where(r >= c, s, NEG)
            m_prev = m_ref[h, rows, :]
            m_new = jnp.maximum(m_prev, jnp.max(s, axis=-1, keepdims=True))
            p = jnp.exp2(s - jnp.concatenate([m_new] * (sub // LANES), axis=1))
            acc_ref[h, rows, :] = (acc_ref[h, rows, :] * jnp.exp2(m_prev - m_new)
                                   + _dot(p.astype(BF16), v_ref[cols, hs]))
            m_ref[h, rows, :] = m_new

    @pl.when(ki < qi)
    def _():
        for ks in range(nsub):
            for qs in range(nsub):
                unit(qs, ks, False)

    @pl.when(ki == qi)
    def _():
        for ks in range(nsub):
            for qs in range(ks, nsub):
                unit(qs, ks, ks == qs)
        outs = []
        for h in range(2):
            a = acc_ref[h]
            outs.append((a * (1.0 / a[:, V_HEAD:V_HEAD + 1]))[:, :V_HEAD])
        o_ref[...] = jnp.concatenate(outs, axis=1).astype(o_ref.dtype)


def _flash_attend(q2, k_all, v_all, batch, t):
    tq = min(ATT_TILE, t)
    nq = t // tq
    pairs = [(qi, ki) for qi in range(nq) for ki in range(qi + 1)]
    qt = jnp.asarray([p[0] for p in pairs], jnp.int32)
    kt = jnp.asarray([p[1] for p in pairs], jnp.int32)
    grid_spec = pltpu.PrefetchScalarGridSpec(
        num_scalar_prefetch=2,
        grid=(batch, MLA_HEADS // 2, len(pairs)),
        in_specs=[pl.BlockSpec((tq, 2 * LANES), lambda b, hp, s, qt, kt: (b * nq + qt[s], hp)),
                  pl.BlockSpec((2 * LANES, tq), lambda b, hp, s, qt, kt: (hp, b * nq + kt[s])),
                  pl.BlockSpec((tq, 2 * LANES), lambda b, hp, s, qt, kt: (b * nq + kt[s], hp))],
        out_specs=pl.BlockSpec((tq, 2 * V_HEAD), lambda b, hp, s, qt, kt: (b * nq + qt[s], hp)),
        scratch_shapes=[pltpu.VMEM((2, tq, LANES), F32), pltpu.VMEM((2, tq, LANES), F32)],
    )
    return pl.pallas_call(
        functools.partial(_flash_kernel, sub=min(ATT_SUB, tq)),
        out_shape=jax.ShapeDtypeStruct((batch * t, MLA_HEADS * V_HEAD), BF16),
        grid_spec=grid_spec,
        compiler_params=_cparams(("parallel", "parallel", "arbitrary")),
        name="mla_flash",
    )(qt, kt, q2, k_all, v_all)


def _decode_kernel(pt_ref, q_ref, cn_ref, pn_ref, ckv_hbm, kpe_hbm, o_ref, ckv_buf, kpe_buf, sem, kv_ref, s_ref,
                   *, n_pages, page, chunk):
    b = pl.program_id(0)
    last = pl.num_programs(0) - 1
    slot = b % 2
    rows = q_ref.shape[1]
    n_chunks = n_pages * page // chunk
    pages_per_chunk = n_pages // n_chunks

    def page_copies(pid, sl, p):
        rows_p = pl.ds(p * page, page)
        return (pltpu.make_async_copy(ckv_hbm.at[pid], ckv_buf.at[sl, rows_p, :], sem.at[0, sl]),
                pltpu.make_async_copy(kpe_hbm.at[pid], kpe_buf.at[sl, :, rows_p], sem.at[1, sl]))

    def start_pages(bb, sl, p0, p1):
        for p in range(p0, p1):
            for cp in page_copies(pt_ref[bb * n_pages + p], sl, p):
                cp.start()

    def wait_pages(sl):
        for p in range(n_pages):
            for cp in page_copies(0, sl, p):
                cp.wait()

    @pl.when(b == 0)
    def _():
        start_pages(0, 0, 0, n_pages)

    wait_pages(slot)
    nxt = jnp.minimum(b + 1, last)

    q = q_ref[0]
    ql = q[:, :KV_LORA]
    qp = q[:, KV_LORA:KV_LORA + QK_ROPE]
    past = n_pages * page

    for c in range(n_chunks):
        start_pages(nxt, 1 - slot, c * pages_per_chunk, (c + 1) * pages_per_chunk)
        cs = slice(c * chunk, (c + 1) * chunk)
        kc = ckv_buf[slot, cs, :].astype(BF16)
        kp = kpe_buf[slot, :, cs].astype(BF16)
        kv_ref[cs, :] = kc
        s_ref[:, cs] = _dot_nt(ql, kc) + _dot(qp, kp)

    t_new = cn_ref.shape[1]
    pad = jnp.zeros((LANES - t_new, KV_LORA), F32)
    kc = jnp.concatenate([cn_ref[0], pad], axis=0).astype(BF16)
    kp = jnp.concatenate([pn_ref[0], pad[:, :LANES]], axis=0).astype(BF16)
    s = _dot_nt(ql, kc) + _dot_nt(q[:, KV_LORA:], kp)
    r = lax.broadcasted_iota(jnp.int32, (rows, LANES), 0)
    c = lax.broadcasted_iota(jnp.int32, (rows, LANES), 1)
    kv_ref[past:, :] = kc
    s_ref[:, past:] = jnp.where(c <= r // MLA_HEADS, s, NEG)

    s_all = s_ref[...]
    p = jnp.exp2(s_all - jnp.max(s_all, axis=-1, keepdims=True))
    l = jnp.sum(p, axis=-1, keepdims=True)
    o_ref[0] = (_dot(p.astype(BF16), kv_ref[...]) / l).astype(o_ref.dtype)

    @pl.when(b == last)
    def _():
        wait_pages(1 - slot)


def _decode_attend(q_abs, ckv_new, kpe_new, cache_ckv, cache_kpe_t, page_table):
    batch, rows, _ = q_abs.shape
    n_pages = page_table.shape[1]
    page = cache_ckv.shape[1]
    past = n_pages * page
    t_new = ckv_new.shape[1]
    per_b = lambda b, pt: (b, 0, 0)
    grid_spec = pltpu.PrefetchScalarGridSpec(
        num_scalar_prefetch=1,
        grid=(batch,),
        in_specs=[pl.BlockSpec((1, rows, LAT_W), per_b),
                  pl.BlockSpec((1, t_new, KV_LORA), per_b),
                  pl.BlockSpec((1, t_new, LANES), per_b),
                  pl.BlockSpec(memory_space=pl.ANY),
                  pl.BlockSpec(memory_space=pl.ANY)],
        out_specs=pl.BlockSpec((1, rows, KV_LORA), per_b),
        scratch_shapes=[pltpu.VMEM((2, past, KV_LORA), F32),
                        pltpu.VMEM((2, QK_ROPE, past), F32),
                        pltpu.SemaphoreType.DMA((2, 2)),
                        pltpu.VMEM((past + LANES, KV_LORA), BF16),
                        pltpu.VMEM((rows, past + LANES), F32)],
    )
    return pl.pallas_call(
        functools.partial(_decode_kernel, n_pages=n_pages, page=page, chunk=min(DECODE_CHUNK, past)),
        out_shape=jax.ShapeDtypeStruct((batch, rows, KV_LORA), BF16),
        grid_spec=grid_spec,
        compiler_params=_cparams(("arbitrary",)),
        name="mla_decode",
    )(page_table.reshape(-1), q_abs, ckv_new, kpe_new, cache_ckv, cache_kpe_t)


def _pad_lanes(v, width):
    return jnp.pad(v, [(0, 0)] * (v.ndim - 1) + [(0, width - v.shape[-1])])


def _ssd_params(i, w_in_a, conv_w, conv_b, dt_bias, a_log, d_skip, g_ssd_norm):
    w = w_in_a[i]
    z0, x0, d0, m0 = 0, D_INNER, D_INNER + CONV_DIM, D_INNER + CONV_DIM + SSD_HEADS
    w_main = jnp.concatenate([w[:, x0:d0], w[:, m0:], w[:, z0:x0]], axis=1).astype(BF16)
    w_dt = _pad_lanes(w[:, d0:m0], LANES).astype(BF16)
    head_of = jnp.arange(D_INNER, dtype=jnp.int32) // SSD_HEAD_DIM
    expand = (jnp.arange(LANES, dtype=jnp.int32)[:, None] == head_of[None, :]).astype(BF16)
    return dict(w_main=w_main, w_dt=w_dt, conv_w=conv_w[i], conv_b=conv_b[i].reshape(1, CONV_DIM),
                dt_bias=_pad_lanes(dt_bias[i].reshape(1, SSD_HEADS), LANES),
                a_log=_pad_lanes(a_log[i].reshape(1, SSD_HEADS), LANES),
                d_x=jnp.repeat(d_skip[i], SSD_HEAD_DIM).reshape(1, D_INNER),
                g_norm=g_ssd_norm[i].reshape(1, D_INNER), expand=expand)


def _mla_shared_params(w_kv_a, w_uk, w_uv):
    half = QK_ROPE // 2
    wa = _pad_lanes(w_kv_a, LAT_W).astype(BF16)
    pe = w_kv_a[:, KV_LORA:]
    wr = _pad_lanes(jnp.concatenate([-pe[:, half:], pe[:, :half]], axis=1), LANES).astype(BF16)
    eye = jnp.eye(QK_ROPE, dtype=F32)
    k_blocks, abs_blocks = [], []
    for h in range(MLA_HEADS):
        kb = jnp.zeros((LAT_W, LANES), F32)
        kb = kb.at[:KV_LORA, :QK_NOPE].set(w_uk[:, h, :])
        kb = kb.at[KV_LORA:KV_LORA + QK_ROPE, QK_NOPE:QK_NOPE + QK_ROPE].set(eye)
        k_blocks.append(kb)
        ab = jnp.zeros((LANES, LAT_W), F32)
        ab = ab.at[:QK_NOPE, :KV_LORA].set(w_uk[:, h, :].T)
        ab = ab.at[QK_NOPE:QK_NOPE + QK_ROPE, KV_LORA:KV_LORA + QK_ROPE].set(eye)
        abs_blocks.append(ab)
    w_k = jnp.concatenate(k_blocks, axis=1).T.astype(BF16)
    v_blocks = []
    for h in range(MLA_HEADS):
        vb = jnp.zeros((LAT_W, LANES), F32).at[:KV_LORA, :V_HEAD].set(w_uv[:, h, :])
        v_blocks.append(vb.at[KV_LORA + QK_ROPE, V_HEAD].set(1.0))
    w_v = jnp.concatenate(v_blocks, axis=1).astype(BF16)
    w_abs = jnp.stack(abs_blocks).astype(BF16)
    zero = jnp.zeros((KV_LORA, V_HEAD), F32)
    w_uvp = jnp.stack([jnp.block([[w_uv[:, 2 * j, :], zero], [zero, w_uv[:, 2 * j + 1, :]]])
                       for j in range(MLA_HEADS // 2)]).astype(BF16)
    return dict(wa=wa, wr=wr, w_k=w_k, w_v=w_v, w_abs=w_abs, w_uvp=w_uvp)


def _mla_layer_params(i, w_in_b, w_q_b):
    half = QK_ROPE // 2
    w = w_in_b[i]
    w_inb = jnp.concatenate([w[:, Q_LORA:], w[:, :Q_LORA]], axis=1).astype(BF16)
    wq = w_q_b[i]
    wqa = _pad_lanes(wq, LANES).reshape(Q_LORA, MLA_HEADS * LANES).astype(BF16)
    rot = jnp.concatenate([jnp.zeros_like(wq[..., :QK_NOPE]), -wq[..., QK_NOPE + half:], wq[..., QK_NOPE:QK_NOPE + half]],
                          axis=-1)
    wqb = _pad_lanes(rot, LANES).reshape(Q_LORA, MLA_HEADS * LANES).astype(BF16)
    return dict(w_inb=w_inb, wqa=wqa, wqb=wqb)


def _rope_tables(pos):
    half = QK_ROPE // 2
    inv = ROPE_BASE ** (-jnp.arange(half, dtype=F32) / half)
    ang = pos.astype(F32)[:, None] * inv[None, :]
    cos, sin = jnp.cos(ang), jnp.sin(ang)
    n = pos.shape[0]
    one = lambda w: jnp.ones((n, w), F32)
    zero = lambda w: jnp.zeros((n, w), F32)
    tail = LANES - QK_NOPE - QK_ROPE
    cos_q = jnp.concatenate([one(QK_NOPE), cos, cos, one(tail)], axis=1)
    sin_q = jnp.concatenate([zero(QK_NOPE), sin, sin, zero(tail)], axis=1)
    cos_k = jnp.concatenate([cos, cos, one(LANES - QK_ROPE)], axis=1)
    sin_k = jnp.concatenate([sin, sin, zero(LANES - QK_ROPE)], axis=1)
    return cos_q, sin_q, cos_k, sin_k


def _trunk(x, batch, t, sample, mem_attend, ssd_state, mla_cache, tabs, P):
    act = F32 if sample else BF16
    cos_q, sin_q, cos_k, sin_k = tabs
    ssm_out, conv_out, stacked = [], [], None
    ckv = kpe = None
    for l in range(DEPTH):
        if l < N_A:
            sp = P["ssd"][l]
            proj = _proj(x, sp["w_main"], g=P["g_pre_mix"][l], out_dtype=act, tn=1024, row_tiles=2, name=f"in_a{l}")
            dt = _proj(x, sp["w_dt"], g=P["g_pre_mix"][l], out_dtype=F32, name=f"in_dt{l}")
            if sample:
                state_ssm, state_conv = ssd_state
                y, stacked = _ssd_sample(proj, dt, state_conv, state_ssm.reshape(N_A, batch, D_INNER, D_STATE),
                                         stacked, l, sp, batch, t)
            else:
                y, h_new = _ssd_prompt(proj, dt, sp, batch, t)
                ssm_out.append(h_new)
                conv_out.append(proj.reshape(batch, t, -1)[:, t - 3:, :CONV_DIM].astype(F32))
            mo = mem_attend(l, proj, 3)
            w_out = P["w_out_a"][l]
            x = _outproj(y, mo, w_out[:D_INNER], w_out[D_INNER:], P["g_post_mix"][l], x, f"out_a{l}")
        else:
            i = l - N_A
            ms, ml = P["mla_shared"], P["mla"][i]
            if i == 0:
                ckv, kpe, lat = _kv_latent(x, P["g_kv_in"], ms["wa"], ms["wr"], P["g_kv_norm"], cos_k, sin_k)
                if not sample:
                    k_all = _proj_t(lat, ms["w_k"], "k_heads")
                    v_all = _proj(lat, ms["w_v"], tn=1024, name="v_heads")
            proj = _proj(x, ml["w_inb"], g=P["g_pre_mix"][l], out_dtype=act, name=f"in_b{i}")
            q2 = _proj(proj, ml["wqa"], g=P["g_q_norm"][i], xcol=2, tn=1024,
                       rope=(ml["wqb"], cos_q, sin_q, MLA_SCALE * LOG2E), name=f"q_heads{i}")
            if sample:
                cache_ckv, cache_kpe, page_table = mla_cache
                q_abs = _block_proj(q2, ms["w_abs"], f"q_abs{i}").reshape(batch, t * MLA_HEADS, LAT_W)
                o_lat = _decode_attend(q_abs, ckv.reshape(batch, t, KV_LORA), kpe.reshape(batch, t, LANES),
                                       cache_ckv, cache_kpe, page_table)
                o = _block_proj(o_lat.reshape(batch * t, MLA_HEADS * KV_LORA), ms["w_uvp"], f"o_heads{i}")
            else:
                o = _flash_attend(q2, k_all, v_all, batch, t)
            mo = mem_attend(l, proj, 0)
            w_out = P["w_out_b"][i]
            n_o = MLA_HEADS * V_HEAD
            x = _outproj(o, mo, w_out[:n_o], w_out[n_o:], P["g_post_mix"][l], x, f"out_b{i}")
        x = _ffn(x, P["g_pre_ffn"][l], P["w_ffn_up"][l], P["w_ffn_down"][l], P["g_post_ffn"][l], f"ffn{l}")
    conv_new, ssm_new = stacked if sample else (jnp.stack(conv_out), jnp.stack(ssm_out))
    ssm_new = ssm_new.reshape(N_A, batch, SSD_HEADS, SSD_HEAD_DIM, D_STATE)
    return x, ssm_new, conv_new, ckv, kpe[:, :QK_ROPE]


def kernel(x_prompt, x_sample, mem_prompt, state_ssm, state_conv, cache_ckv, cache_kpe, cache_mem_k, cache_mem_v, page_table, g_pre_mix, g_post_mix, g_pre_ffn, g_post_ffn, w_ffn_up, w_ffn_down, w_mem_k, w_mem_v, w_in_a, conv_w, conv_b, dt_bias, a_log, d_skip, g_ssd_norm, w_out_a, g_kv_in, w_kv_a, g_kv_norm, w_uk, w_uv, w_in_b, g_q_norm, w_q_b, w_out_b):
    b, s, _ = x_prompt.shape
    db, t, _ = x_sample.shape
    past_len = page_table.shape[1] * cache_ckv.shape[1]

    P = dict(
        g_pre_mix=g_pre_mix, g_post_mix=g_post_mix, g_pre_ffn=g_pre_ffn, g_post_ffn=g_post_ffn,
        g_kv_in=g_kv_in, g_kv_norm=g_kv_norm, g_q_norm=g_q_norm,
        w_ffn_up=w_ffn_up.astype(BF16), w_ffn_down=w_ffn_down.astype(BF16),
        w_out_a=w_out_a.astype(BF16), w_out_b=w_out_b.astype(BF16),
        ssd=[_ssd_params(i, w_in_a, conv_w, conv_b, dt_bias, a_log, d_skip, g_ssd_norm) for i in range(N_A)],
        mla_shared=_mla_shared_params(w_kv_a, w_uk, w_uv),
        mla=[_mla_layer_params(i, w_in_b, w_q_b) for i in range(DEPTH - N_A)],
    )

    w_mem = jnp.concatenate([w_mem_k[l] for l in range(DEPTH)] + [w_mem_v[l] for l in range(DEPTH)], axis=1)
    mem_kv = _proj(mem_prompt.reshape(b * N_MEM, D_MODEL), w_mem.astype(BF16), out_dtype=F32, tn=1024, name="mem_kv")
    mem_kv5 = mem_kv.reshape(b, N_MEM, 2 * DEPTH, MEM_HEADS, MEM_HEAD_DIM)
    p_mem_k = jnp.transpose(mem_kv5[:, :, :DEPTH], (2, 0, 1, 3, 4))
    p_mem_v = jnp.transpose(mem_kv5[:, :, DEPTH:], (2, 0, 1, 3, 4))
    mem_kv3 = mem_kv.reshape(b, N_MEM, 2 * DEPTH * MEM_WIDTH)
    tabs_p = _rope_tables(jnp.arange(s, dtype=jnp.int32))
    y_prompt, p_ssm, p_conv, p_ckv, p_kpe = _trunk(
        x_prompt.reshape(b * s, D_MODEL), b, s, False,
        lambda l, q, qcol: _mem_attend(q, qcol, mem_kv3, mem_kv3, 0, l, DEPTH + l, b, s, BF16, f"mem{l}"),
        None, None, tabs_p, P)

    tabs_s = tuple(jnp.tile(tb, (db, 1)) for tb in _rope_tables(past_len + jnp.arange(t, dtype=jnp.int32)))
    cache_kpe_t = jnp.swapaxes(cache_kpe, 1, 2)
    y_sample, s_ssm, s_conv, s_ckv, s_kpe = _trunk(
        x_sample.reshape(db * t, D_MODEL), db, t, True,
        lambda l, q, qcol: _mem_attend_cache(q, qcol, cache_mem_k, cache_mem_v, l, db, t, f"mem_s{l}"),
        (state_ssm, state_conv), (cache_ckv, cache_kpe_t, page_table), tabs_s, P)

    return (y_prompt.reshape(b, s, D_MODEL), y_sample.reshape(db, t, D_MODEL),
            p_ssm, p_conv, p_ckv.reshape(b, s, KV_LORA), p_kpe.reshape(b, s, QK_ROPE), p_mem_k, p_mem_v,
            s_ssm, s_conv, s_ckv.reshape(db, t, KV_LORA), s_kpe.reshape(db, t, QK_ROPE))
```

```python
import functools

import jax
import jax.numpy as jnp
from jax import lax
from jax.experimental import pallas as pl
from jax.experimental.pallas import tpu as pltpu

F32 = jnp.float32
BF16 = jnp.bfloat16

D_MODEL = 1024
DEPTH = 4
N_A = 2
D_INNER = 2048
SSD_HEADS = 32
SSD_HEAD_DIM = 64
SSD_GROUPS = 4
GROUP_W = D_INNER // SSD_GROUPS
D_STATE = 128
CONV_DIM = 3072
SSD_CHUNK = 128
MLA_HEADS = 16
Q_LORA = 512
KV_LORA = 256
QK_NOPE = 64
QK_ROPE = 32
V_HEAD = 64
ROPE_BASE = 10000.0
MLA_SCALE = (QK_NOPE + QK_ROPE) ** -0.5
LOG2E = 1.4426950408889634
N_MEM = 256
MEM_HEADS = 4
MEM_HEAD_DIM = 256
MEM_WIDTH = 1024
D_FF = 4096
RMS_EPS = 1e-6
LANES = 128
LAT_W = 384
NEG = -1e30
ROW_TILE = 512
ATT_TILE = 1024
ATT_SUB = 512
FLASH_HEADS_PER_STEP = 8
DECODE_CHUNK = 2048
MEM_BATCH_BLOCK = 4

NT_DIMS = (((1,), (1,)), ((), ()))
TN_DIMS = (((0,), (0,)), ((), ()))


def _cparams(sem, vmem_mb=48):
    return pltpu.CompilerParams(dimension_semantics=sem, vmem_limit_bytes=vmem_mb * 1024 * 1024)


def _rms(x, g):
    ms = jnp.mean(x * x, axis=-1, keepdims=True)
    return x * lax.rsqrt(ms + RMS_EPS) * g


def _silu(x):
    return x * (1.0 / (1.0 + jnp.exp(-x)))


def _softplus(x):
    return jnp.maximum(x, 0.0) + jnp.log1p(jnp.exp(-jnp.abs(x)))


def _dot(a, b):
    return jnp.dot(a, b, preferred_element_type=F32)


def _dot_nt(a, b):
    return lax.dot_general(a, b, NT_DIMS, preferred_element_type=F32)


def _dot_tn(a, b):
    return lax.dot_general(a, b, TN_DIMS, preferred_element_type=F32)


def _split_bf16(v, terms):
    parts = []
    for _ in range(terms):
        p = v.astype(BF16)
        parts.append(p)
        v = v - p.astype(F32)
    return parts


def _sel_dot(m, v, terms=2):
    return sum(_dot(m, p) for p in _split_bf16(v, terms))


def _expand(v, e_ref, terms=2):
    return sum(_dot(p, e_ref[...]) for p in _split_bf16(v, terms))


def _proj_kernel(*refs, has_g, has_rope, scale):
    it = iter(refs)
    x_ref = next(it)
    g_ref = next(it) if has_g else None
    w_ref = next(it)
    if has_rope:
        w2_ref, cos_ref, sin_ref = next(it), next(it), next(it)
    o_ref = next(it)
    xn_ref = next(it)

    @pl.when(pl.program_id(1) == 0)
    def _():
        x = x_ref[...].astype(F32)
        if has_g:
            x = _rms(x, g_ref[...])
        xn_ref[...] = x.astype(BF16)

    xn = xn_ref[...]
    y = _dot(xn, w_ref[...])
    if has_rope:
        rep = y.shape[1] // LANES
        cos = jnp.concatenate([cos_ref[...]] * rep, axis=1)
        sin = jnp.concatenate([sin_ref[...]] * rep, axis=1)
        y = (y * cos + _dot(xn, w2_ref[...]) * sin) * scale
    if len(o_ref.shape) == 3:
        for hh in range(o_ref.shape[0]):
            o_ref[hh] = y[:, hh * LANES:(hh + 1) * LANES].astype(o_ref.dtype)
    else:
        o_ref[...] = y.astype(o_ref.dtype)


def _proj(x, w, *, g=None, xcol=0, out_dtype=BF16, tn=None, rope=None, row_tiles=1, head_major=False, name):
    n = x.shape[0]
    k, nout = w.shape
    tm = min(ROW_TILE * row_tiles, n)
    tn = nout if tn is None else tn
    if head_major:
        out_shape = jax.ShapeDtypeStruct((nout // LANES, n, LANES), out_dtype)
        out_spec = pl.BlockSpec((tn // LANES, tm, LANES), lambda i, j: (j, i, 0))
    else:
        out_shape = jax.ShapeDtypeStruct((n, nout), out_dtype)
        out_spec = pl.BlockSpec((tm, tn), lambda i, j: (i, j))
    in_specs = [pl.BlockSpec((tm, k), lambda i, j: (i, xcol))]
    args = [x]
    if g is not None:
        in_specs.append(pl.BlockSpec((1, k), lambda i, j: (0, 0)))
        args.append(g.reshape(1, k).astype(F32))
    in_specs.append(pl.BlockSpec((k, tn), lambda i, j: (0, j)))
    args.append(w)
    scale = 1.0
    if rope is not None:
        w2, cos, sin, scale = rope
        nb = cos.shape[0] // tm
        in_specs.append(pl.BlockSpec((k, tn), lambda i, j: (0, j)))
        in_specs.append(pl.BlockSpec((tm, LANES), lambda i, j: (i % nb, 0)))
        in_specs.append(pl.BlockSpec((tm, LANES), lambda i, j: (i % nb, 0)))
        args += [w2, cos, sin]
    return pl.pallas_call(
        functools.partial(_proj_kernel, has_g=g is not None, has_rope=rope is not None, scale=scale),
        out_shape=out_shape,
        grid=(n // tm, nout // tn),
        in_specs=in_specs,
        out_specs=out_spec,
        scratch_shapes=[pltpu.VMEM((tm, k), BF16)],
        compiler_params=_cparams(("parallel", "arbitrary")),
        name=name,
    )(*args)


def _proj_t_kernel(x_ref, w_ref, o_ref):
    o_ref[...] = _dot_nt(w_ref[...], x_ref[...]).astype(o_ref.dtype)


def _proj_t(x, w_t, name):
    n, k = x.shape
    nout = w_t.shape[0]
    tm = min(ROW_TILE, n)
    return pl.pallas_call(
        _proj_t_kernel,
        out_shape=jax.ShapeDtypeStruct((nout, n), BF16),
        grid=(n // tm,),
        in_specs=[pl.BlockSpec((tm, k), lambda i: (i, 0)), pl.BlockSpec((nout, k), lambda i: (0, 0))],
        out_specs=pl.BlockSpec((nout, tm), lambda i: (0, i)),
        compiler_params=_cparams(("parallel",)),
        name=name,
    )(x, w_t)


def _block_proj_kernel(x_ref, w_ref, o_ref):
    o_ref[...] = _dot(x_ref[...].astype(BF16), w_ref[0]).astype(o_ref.dtype)


def _block_proj(x, w, name):
    n = x.shape[0]
    nb, kx, ko = w.shape
    tm = min(ROW_TILE, n)
    return pl.pallas_call(
        _block_proj_kernel,
        out_shape=jax.ShapeDtypeStruct((n, nb * ko), BF16),
        grid=(n // tm, nb),
        in_specs=[pl.BlockSpec((tm, kx), lambda i, j: (i, j)), pl.BlockSpec((1, kx, ko), lambda i, j: (j, 0, 0))],
        out_specs=pl.BlockSpec((tm, ko), lambda i, j: (i, j)),
        compiler_params=_cparams(("parallel", "parallel")),
        name=name,
    )(x, w)


def _kvlat_kernel(x_ref, g_ref, wa_ref, wr_ref, gkv_ref, cos_ref, sin_ref, ckv_ref, kpe_ref, lat_ref):
    xn = _rms(x_ref[...], g_ref[...]).astype(BF16)
    a = _dot(xn, wa_ref[...])
    r = _dot(xn, wr_ref[...])
    ckv = _rms(a[:, :KV_LORA], gkv_ref[...])
    pe = a[:, KV_LORA:] * cos_ref[...] + r * sin_ref[...]
    ckv_ref[...] = ckv
    kpe_ref[...] = pe
    lat_ref[:, :KV_LORA] = ckv.astype(BF16)
    lane = lax.broadcasted_iota(jnp.int32, pe.shape, 1)
    lat_ref[:, KV_LORA:] = jnp.where(lane == QK_ROPE, 1.0, pe).astype(BF16)


def _kv_latent(x, g, wa, wr, gkv, cos, sin):
    n = x.shape[0]
    tm = min(ROW_TILE, n)
    nb = cos.shape[0] // tm
    row = lambda i: (i, 0)
    const = lambda i: (0, 0)
    return pl.pallas_call(
        _kvlat_kernel,
        out_shape=(jax.ShapeDtypeStruct((n, KV_LORA), F32),
                   jax.ShapeDtypeStruct((n, LANES), F32),
                   jax.ShapeDtypeStruct((n, LAT_W), BF16)),
        grid=(n // tm,),
        in_specs=[pl.BlockSpec((tm, D_MODEL), row),
                  pl.BlockSpec((1, D_MODEL), const),
                  pl.BlockSpec((D_MODEL, LAT_W), const),
                  pl.BlockSpec((D_MODEL, LANES), const),
                  pl.BlockSpec((1, KV_LORA), const),
                  pl.BlockSpec((tm, LANES), lambda i: (i % nb, 0)),
                  pl.BlockSpec((tm, LANES), lambda i: (i % nb, 0))],
        out_specs=(pl.BlockSpec((tm, KV_LORA), row),
                   pl.BlockSpec((tm, LANES), row),
                   pl.BlockSpec((tm, LAT_W), row)),
        compiler_params=_cparams(("parallel",)),
        name="kv_latent",
    )(x, g.reshape(1, D_MODEL), wa, wr, gkv.reshape(1, KV_LORA), cos, sin)


def _outproj_kernel(a1_ref, a2_ref, w1_ref, w2_ref, g_ref, r_ref, o_ref):
    acc = _dot(a1_ref[...].astype(BF16), w1_ref[...]) + _dot(a2_ref[...].astype(BF16), w2_ref[...])
    o_ref[...] = r_ref[...] + _rms(acc, g_ref[...])


def _outproj(a1, a2, w1, w2, g, res, name):
    n = res.shape[0]
    tm = min(ROW_TILE, n)
    k1, k2 = w1.shape[0], w2.shape[0]
    row = lambda i: (i, 0)
    const = lambda i: (0, 0)
    return pl.pallas_call(
        _outproj_kernel,
        out_shape=jax.ShapeDtypeStruct((n, D_MODEL), F32),
        grid=(n // tm,),
        in_specs=[pl.BlockSpec((tm, k1), row), pl.BlockSpec((tm, k2), row),
                  pl.BlockSpec((k1, D_MODEL), const), pl.BlockSpec((k2, D_MODEL), const),
                  pl.BlockSpec((1, D_MODEL), const), pl.BlockSpec((tm, D_MODEL), row)],
        out_specs=pl.BlockSpec((tm, D_MODEL), row),
        compiler_params=_cparams(("parallel",)),
        name=name,
    )(a1, a2, w1, w2, g.reshape(1, D_MODEL), res)


def _ffn_kernel(x_ref, g1_ref, wu_ref, wd_ref, g2_ref, o_ref):
    x = x_ref[...]
    xn = _rms(x, g1_ref[...]).astype(BF16)
    acc = jnp.zeros(x.shape, F32)
    for c in range(D_FF // D_MODEL):
        sl = slice(c * D_MODEL, (c + 1) * D_MODEL)
        h = _dot(xn, wu_ref[:, sl])
        h = jnp.square(jnp.maximum(h, 0.0)).astype(BF16)
        acc = acc + _dot(h, wd_ref[sl, :])
    o_ref[...] = x + _rms(acc, g2_ref[...])


def _ffn(x, g1, wu, wd, g2, name):
    n = x.shape[0]
    tm = min(ROW_TILE, n)
    row = lambda i: (i, 0)
    const = lambda i: (0, 0)
    return pl.pallas_call(
        _ffn_kernel,
        out_shape=jax.ShapeDtypeStruct((n, D_MODEL), F32),
        grid=(n // tm,),
        in_specs=[pl.BlockSpec((tm, D_MODEL), row), pl.BlockSpec((1, D_MODEL), const),
                  pl.BlockSpec((D_MODEL, D_FF), const, pipeline_mode=pl.Buffered(1)),
                  pl.BlockSpec((D_FF, D_MODEL), const, pipeline_mode=pl.Buffered(1)),
                  pl.BlockSpec((1, D_MODEL), const)],
        out_specs=pl.BlockSpec((tm, D_MODEL), row),
        compiler_params=_cparams(("parallel",), vmem_mb=48),
        name=name,
    )(x, g1.reshape(1, D_MODEL), wu, wd, g2.reshape(1, D_MODEL))


def _mem_head(qh, kh, vh):
    s = _dot_nt(qh.astype(BF16), kh.astype(BF16)) * (MEM_HEAD_DIM ** -0.5)
    m = jnp.max(s, axis=-1, keepdims=True)
    p = jnp.exp(s - m)
    l = jnp.sum(p, axis=-1, keepdims=True)
    return _dot(p.astype(BF16), vh.astype(BF16)) / l


def _mem_kernel(q_ref, k_ref, v_ref, o_ref):
    q = q_ref[...]
    for h in range(MEM_HEADS):
        sl = slice(h * MEM_HEAD_DIM, (h + 1) * MEM_HEAD_DIM)
        o_ref[:, sl] = _mem_head(q[:, sl], k_ref[0, :, sl], v_ref[0, :, sl]).astype(o_ref.dtype)


def _mem_cache_kernel(q_ref, k_hbm, v_hbm, o_ref, kbuf, vbuf, sem, *, layer, bb, t):
    step = pl.program_id(0)
    slot = step % 2

    def copies(st, sl):
        out = []
        for i in range(bb):
            for h in range(MEM_HEADS):
                out.append(pltpu.make_async_copy(k_hbm.at[layer, st * bb + i, :, h, :], kbuf.at[sl, i, h], sem.at[0, sl]))
                out.append(pltpu.make_async_copy(v_hbm.at[layer, st * bb + i, :, h, :], vbuf.at[sl, i, h], sem.at[1, sl]))
        return out

    @pl.when(step == 0)
    def _():
        for cp in copies(0, 0):
            cp.start()

    @pl.when(step + 1 < pl.num_programs(0))
    def _():
        for cp in copies(step + 1, 1 - slot):
            cp.start()

    for cp in copies(step, slot):
        cp.wait()

    r = lax.broadcasted_iota(jnp.int32, (MEM_HEADS * t, MEM_HEADS * N_MEM), 0)
    c = lax.broadcasted_iota(jnp.int32, (MEM_HEADS * t, MEM_HEADS * N_MEM), 1)
    same_head = (r // t) == (c // N_MEM)
    for i in range(bb):
        rows = slice(i * t, (i + 1) * t)
        qs = jnp.concatenate([q_ref[rows, h * MEM_HEAD_DIM:(h + 1) * MEM_HEAD_DIM] for h in range(MEM_HEADS)], axis=0)
        ks = jnp.concatenate([kbuf[slot, i, h].astype(BF16) for h in range(MEM_HEADS)], axis=0)
        vs = jnp.concatenate([vbuf[slot, i, h].astype(BF16) for h in range(MEM_HEADS)], axis=0)
        s = jnp.where(same_head, _dot_nt(qs.astype(BF16), ks) * (MEM_HEAD_DIM ** -0.5), NEG)
        p = jnp.exp(s - jnp.max(s, axis=-1, keepdims=True))
        o = _dot(p.astype(BF16), vs) / jnp.sum(p, axis=-1, keepdims=True)
        for h in range(MEM_HEADS):
            o_ref[rows, h * MEM_HEAD_DIM:(h + 1) * MEM_HEAD_DIM] = o[h * t:(h + 1) * t, :].astype(o_ref.dtype)


def _mem_attend_cache(q, qcol, cache_k, cache_v, layer, batch, t, name):
    bb = min(MEM_BATCH_BLOCK, batch)
    buf = pltpu.VMEM((2, bb, MEM_HEADS, N_MEM, MEM_HEAD_DIM), F32)
    return pl.pallas_call(
        functools.partial(_mem_cache_kernel, layer=layer, bb=bb, t=t),
        out_shape=jax.ShapeDtypeStruct((batch * t, MEM_WIDTH), F32),
        grid=(batch // bb,),
        in_specs=[pl.BlockSpec((bb * t, MEM_WIDTH), lambda i: (i, qcol)),
                  pl.BlockSpec(memory_space=pl.ANY), pl.BlockSpec(memory_space=pl.ANY)],
        out_specs=pl.BlockSpec((bb * t, MEM_WIDTH), lambda i: (i, 0)),
        scratch_shapes=[buf, buf, pltpu.SemaphoreType.DMA((2, 2))],
        compiler_params=_cparams(("arbitrary",)),
        name=name,
    )(q, cache_k, cache_v)


def _mem_attend(q, qcol, k_arr, v_arr, kboff, kcol, vcol, batch, t, out_dtype, name):
    tq = min(ROW_TILE, t)
    nt = t // tq
    return pl.pallas_call(
        _mem_kernel,
        out_shape=jax.ShapeDtypeStruct((batch * t, MEM_WIDTH), out_dtype),
        grid=(batch, nt),
        in_specs=[pl.BlockSpec((tq, MEM_WIDTH), lambda b, i: (b * nt + i, qcol)),
                  pl.BlockSpec((1, N_MEM, MEM_WIDTH), lambda b, i: (b + kboff, 0, kcol)),
                  pl.BlockSpec((1, N_MEM, MEM_WIDTH), lambda b, i: (b + kboff, 0, vcol))],
        out_specs=pl.BlockSpec((tq, MEM_WIDTH), lambda b, i: (b * nt + i, 0)),
        compiler_params=_cparams(("parallel", "parallel")),
        name=name,
    )(q, k_arr, v_arr)


def _gate_norm_store(y_ref, ygrp, z_ref, gn_ref, g):
    sl = slice(g * GROUP_W, (g + 1) * GROUP_W)
    ygrp = ygrp * _silu(z_ref[:, sl].astype(F32))
    y_ref[:, sl] = _rms(ygrp, gn_ref[:, sl]).astype(y_ref.dtype)


def _ssd_prompt_kernel(xs_ref, bc_ref, z_ref, dt_ref, cw_ref, cb_ref, dtb_ref, alog_ref, d_ref, gn_ref, e_ref, sh_ref,
                       y_ref, hout_ref, extx, extbc, ht):
    q = SSD_CHUNK
    c = pl.program_id(1)
    tail = 16

    @pl.when(c == 0)
    def _():
        extx[0:q] = jnp.zeros((q, D_INNER), BF16)
        extbc[0:q] = jnp.zeros((q, CONV_DIM - D_INNER), BF16)
        ht[...] = jnp.zeros(ht.shape, F32)

    @pl.when(c > 0)
    def _():
        extx[q - tail:q] = extx[2 * q - tail:2 * q]
        extbc[q - tail:q] = extbc[2 * q - tail:2 * q]

    extx[q:2 * q] = xs_ref[...]
    extbc[q:2 * q] = bc_ref[...]

    def conv(ext, w, b):
        sh = _dot(sh_ref[...], ext[...])
        acc = (b + w[3:4] * ext[q:2 * q].astype(F32) + w[2:3] * sh[0:q] + w[1:2] * sh[q:2 * q]
               + w[0:1] * sh[2 * q:3 * q])
        return _silu(acc)

    cw = cw_ref[...]
    cb = cb_ref[...]
    xs = conv(extx, cw[:, :D_INNER], cb[:, :D_INNER])
    bcv = conv(extbc, cw[:, D_INNER:], cb[:, D_INNER:])

    dt = _softplus(dt_ref[...] + dtb_ref[...])
    a = dt * (-jnp.exp(alog_ref[...]))
    row = lax.broadcasted_iota(jnp.int32, (q, q), 0)
    col = lax.broadcasted_iota(jnp.int32, (q, q), 1)
    causal = row >= col
    acs = _sel_dot(jnp.where(causal, 1.0, 0.0).astype(BF16), a, terms=3)
    acs_t = acs.T
    a_last = acs[q - 1:q, :]
    dtx = _expand(dt, e_ref)
    dwx = _expand(dt * jnp.exp(a_last - acs), e_ref)
    eax = _expand(jnp.exp(acs), e_ref)
    xdt = xs * dtx
    xw_b = (xs * dwx).astype(BF16)
    lane = lax.broadcasted_iota(jnp.int32, (q, LANES), 1)
    heads_per_group = SSD_HEADS // SSD_GROUPS

    for g in range(SSD_GROUPS):
        gs = slice(g * GROUP_W, (g + 1) * GROUP_W)
        bg = bcv[:, g * D_STATE:(g + 1) * D_STATE].astype(BF16)
        cg = bcv[:, (SSD_GROUPS + g) * D_STATE:(SSD_GROUPS + g + 1) * D_STATE].astype(BF16)
        cbm = _dot_nt(cg, bg)
        htg = ht[:, gs]
        y_off = _dot(cg, htg.astype(BF16)) * eax[:, gs]
        ht[:, gs] = eax[q - 1:q, gs] * htg + _dot_tn(bg, xw_b[:, gs])
        pairs = []
        for j in range(heads_per_group // 2):
            h0 = g * heads_per_group + 2 * j
            xpair = xdt[:, h0 * SSD_HEAD_DIM:(h0 + 2) * SSD_HEAD_DIM]
            yp = None
            for k in range(2):
                h = h0 + k
                seg = acs[:, h:h + 1] - acs_t[h:h + 1, :]
                m = (cbm * jnp.exp(jnp.where(causal, seg, NEG))).astype(BF16)
                keep = (lane < SSD_HEAD_DIM) if k == 0 else (lane >= SSD_HEAD_DIM)
                t = _dot(m, jnp.where(keep, xpair, 0.0).astype(BF16))
                yp = t if yp is None else yp + t
            pairs.append(yp)
        ygrp = jnp.concatenate(pairs, axis=1) + y_off + d_ref[:, gs] * xs[:, gs]
        _gate_norm_store(y_ref, ygrp, z_ref, gn_ref, g)

    @pl.when(c == pl.num_programs(1) - 1)
    def _():
        hout_ref[0] = ht[...].T


def _ssd_prompt(proj, dt, sp, batch, t):
    nc = t // SSD_CHUNK
    q = SSD_CHUNK
    const = lambda b, c: (0, 0)
    return pl.pallas_call(
        _ssd_prompt_kernel,
        out_shape=(jax.ShapeDtypeStruct((batch * t, D_INNER), BF16),
                   jax.ShapeDtypeStruct((batch, D_INNER, D_STATE), F32)),
        grid=(batch, nc),
        in_specs=[pl.BlockSpec((q, D_INNER), lambda b, c: (b * nc + c, 0)),
                  pl.BlockSpec((q, 1024), lambda b, c: (b * nc + c, 2)),
                  pl.BlockSpec((q, D_INNER), lambda b, c: (b * nc + c, 2)),
                  pl.BlockSpec((q, LANES), lambda b, c: (b * nc + c, 0)),
                  pl.BlockSpec((4, CONV_DIM), const), pl.BlockSpec((1, CONV_DIM), const),
                  pl.BlockSpec((1, LANES), const), pl.BlockSpec((1, LANES), const),
                  pl.BlockSpec((1, D_INNER), const), pl.BlockSpec((1, D_INNER), const),
                  pl.BlockSpec((LANES, D_INNER), const),
                  pl.BlockSpec((3 * q, 2 * q), const)],
        out_specs=(pl.BlockSpec((q, D_INNER), lambda b, c: (b * nc + c, 0)),
                   pl.BlockSpec((1, D_INNER, D_STATE), lambda b, c: (b, 0, 0))),
        scratch_shapes=[pltpu.VMEM((2 * q, D_INNER), BF16),
                        pltpu.VMEM((2 * q, CONV_DIM - D_INNER), BF16),
                        pltpu.VMEM((D_STATE, D_INNER), F32)],
        compiler_params=_cparams(("parallel", "arbitrary"), vmem_mb=48),
        name="ssd_prompt",
    )(proj, proj, proj, dt, sp["conv_w"], sp["conv_b"], sp["dt_bias"], sp["a_log"], sp["d_x"], sp["g_norm"], sp["expand"],
      _conv_shift_matrix(q))


def _conv_shift_matrix(q):
    row = jnp.arange(3 * q, dtype=jnp.int32)[:, None]
    col = jnp.arange(2 * q, dtype=jnp.int32)[None, :]
    return (col == q + row % q - (row // q + 1)).astype(BF16)


def _ssd_sample_kernel(xs_ref, bc_ref, z_ref, dt_ref, cs_ref, h0_ref, cw_ref, cb_ref, dtb_ref, alog_ref, d_ref,
                       gn_ref, e_ref, *rest, n_carried, out_layer):
    y_ref, cnew_ref, hnew_ref, ext = rest[n_carried:]
    t = xs_ref.shape[0]
    for other in range(cnew_ref.shape[0]):
        if other != out_layer:
            cnew_ref[other] = jnp.zeros(cnew_ref.shape[1:], F32)
            hnew_ref[other] = jnp.zeros(hnew_ref.shape[1:], F32)
    ext[5:8] = cs_ref[0]
    ext[8:8 + t, 0:D_INNER] = xs_ref[...]
    ext[8:8 + t, D_INNER:CONV_DIM] = bc_ref[...]
    cw = cw_ref[...]
    acc = (cb_ref[...] + cw[0:1] * ext[5:5 + t] + cw[1:2] * ext[6:6 + t]
           + cw[2:3] * ext[7:7 + t] + cw[3:4] * ext[8:8 + t])
    cnew_ref[out_layer, 0] = ext[5 + t:8 + t]
    xbc = _silu(acc)
    xs = xbc[:, :D_INNER]

    dt = _softplus(dt_ref[...] + dtb_ref[...])
    a = dt * (-jnp.exp(alog_ref[...]))
    rowi = lax.broadcasted_iota(jnp.int32, (t, LANES), 0)
    acs = jnp.zeros((t, LANES), F32)
    for s in range(t):
        acs = acs + jnp.where(rowi >= s, a[s:s + 1, :], 0.0)
    a_last = acs[t - 1:t, :]
    eacs = jnp.exp(acs)
    acsx = _expand(acs, e_ref, terms=3)
    dtx = _expand(dt, e_ref)
    dwx = _expand(dt * jnp.exp(a_last - acs), e_ref)
    eax = _expand(eacs, e_ref)
    xdt = xs * dtx
    xw_b = (xs * dwx).astype(BF16)

    def grp(base, g):
        return xbc[:, D_INNER + (base + g) * D_STATE:D_INNER + (base + g + 1) * D_STATE]

    li = lax.broadcasted_iota(jnp.int32, (t, D_INNER), 0)
    y = jnp.zeros((t, D_INNER), F32)
    for s in range(t):
        decay = jnp.exp(jnp.where(li >= s, acsx - acsx[s:s + 1, :], NEG))
        cbx = jnp.concatenate(
            [jnp.broadcast_to(jnp.sum(grp(SSD_GROUPS, g) * grp(0, g)[s:s + 1, :], axis=-1, keepdims=True),
                              (t, GROUP_W)) for g in range(SSD_GROUPS)], axis=1)
        y = y + cbx * decay * xdt[s:s + 1, :]

    last_only = jnp.where(rowi == t - 1, eacs, 0.0)
    ones = jnp.ones((t, LANES), BF16)
    rdec = sum(_dot_tn(p, ones) for p in _split_bf16(last_only, 3))

    heads_per_group = SSD_HEADS // SSD_GROUPS
    for g in range(SSD_GROUPS):
        gs = slice(g * GROUP_W, (g + 1) * GROUP_W)
        bg = grp(0, g).astype(BF16)
        cg = grp(SSD_GROUPS, g).astype(BF16)
        h0g = h0_ref[0, gs, :]
        y_off = _dot_nt(cg, h0g.astype(BF16)) * eax[:, gs]
        upd = _dot_tn(xw_b[:, gs], bg)
        for hh in range(heads_per_group):
            h = g * heads_per_group + hh
            rs = slice(hh * SSD_HEAD_DIM, (hh + 1) * SSD_HEAD_DIM)
            hnew_ref[out_layer, 0, h * SSD_HEAD_DIM:(h + 1) * SSD_HEAD_DIM, :] = (
                rdec[h:h + 1, :] * h0g[rs, :] + upd[rs, :])
        ygrp = y[:, gs] + y_off + d_ref[:, gs] * xs[:, gs]
        _gate_norm_store(y_ref, ygrp, z_ref, gn_ref, g)


def _ssd_sample(proj, dt, conv_all, ssm_all, stacked, layer, sp, batch, t):
    const = lambda b: (0, 0)
    n_layers = ssm_all.shape[0]
    conv_spec = pl.BlockSpec((None, 1, 3, CONV_DIM), lambda b: (layer, b, 0, 0))
    ssm_spec = pl.BlockSpec((None, 1, D_INNER, D_STATE), lambda b: (layer, b, 0, 0))
    in_specs = [pl.BlockSpec((t, D_INNER), lambda b: (b, 0)),
                pl.BlockSpec((t, 1024), lambda b: (b, 2)),
                pl.BlockSpec((t, D_INNER), lambda b: (b, 2)),
                pl.BlockSpec((t, LANES), lambda b: (b, 0)),
                conv_spec, ssm_spec,
                pl.BlockSpec((4, CONV_DIM), const), pl.BlockSpec((1, CONV_DIM), const),
                pl.BlockSpec((1, LANES), const), pl.BlockSpec((1, LANES), const),
                pl.BlockSpec((1, D_INNER), const), pl.BlockSpec((1, D_INNER), const),
                pl.BlockSpec((LANES, D_INNER), const)]
    args = [proj, proj, proj, dt, conv_all, ssm_all, sp["conv_w"], sp["conv_b"], sp["dt_bias"], sp["a_log"],
            sp["d_x"], sp["g_norm"], sp["expand"]]
    if stacked is None:
        aliases = {}
        out_layer = layer
        conv_out = pl.BlockSpec((n_layers, 1, 3, CONV_DIM), lambda b: (0, b, 0, 0))
        ssm_out = pl.BlockSpec((n_layers, 1, D_INNER, D_STATE), lambda b: (0, b, 0, 0))
    else:
        aliases = {len(args): 1, len(args) + 1: 2}
        in_specs += [pl.BlockSpec(memory_space=pl.ANY), pl.BlockSpec(memory_space=pl.ANY)]
        args += list(stacked)
        out_layer = 0
        conv_out = pl.BlockSpec((1, 1, 3, CONV_DIM), lambda b: (layer, b, 0, 0))
        ssm_out = pl.BlockSpec((1, 1, D_INNER, D_STATE), lambda b: (layer, b, 0, 0))
    y, conv_new, ssm_new = pl.pallas_call(
        functools.partial(_ssd_sample_kernel, n_carried=len(aliases), out_layer=out_layer),
        out_shape=(jax.ShapeDtypeStruct((batch * t, D_INNER), F32),
                   jax.ShapeDtypeStruct((n_layers, batch, 3, CONV_DIM), F32),
                   jax.ShapeDtypeStruct((n_layers, batch, D_INNER, D_STATE), F32)),
        grid=(batch,),
        in_specs=in_specs,
        out_specs=(pl.BlockSpec((t, D_INNER), lambda b: (b, 0)), conv_out, ssm_out),
        scratch_shapes=[pltpu.VMEM((8 + t, CONV_DIM), F32)],
        input_output_aliases=aliases,
        compiler_params=_cparams(("parallel",)),
        name="ssd_sample",
    )(*args)
    return y, (conv_new, ssm_new)


def _flash_kernel(qt_ref, kt_ref, q_ref, k_ref, v_ref, o_ref, m_ref, acc_ref, *, sub):
    step = pl.program_id(2)
    qi = qt_ref[step]
    ki = kt_ref[step]
    n_heads, tq, _ = q_ref.shape
    nsub = tq // sub

    @pl.when(ki == 0)
    def _():
        m_ref[...] = jnp.full(m_ref.shape, NEG, F32)
        acc_ref[...] = jnp.zeros(acc_ref.shape, F32)

    def unit(pair, qs, ks, masked):
        rows = slice(qs * sub, (qs + 1) * sub)
        cols = slice(ks * sub, (ks + 1) * sub)
        for h in range(2):
            hd = pair * 2 + h
            s = _dot(q_ref[hd, rows, :], k_ref[pl.ds(pl.multiple_of(hd * LANES, LANES), LANES), cols])
            if masked:
                r = lax.broadcasted_iota(jnp.int32, (sub, sub), 0)
                c = lax.broadcasted_iota(jnp.int32, (sub, sub), 1)
                s = jnp.where(r >= c, s, NEG)
            m_prev = m_ref[hd, rows, :]
            m_new = jnp.maximum(m_prev, jnp.max(s, axis=-1, keepdims=True))
            p = jnp.exp2(s - jnp.concatenate([m_new] * (sub // LANES), axis=1))
            acc_ref[hd, rows, :] = (acc_ref[hd, rows, :] * jnp.exp2(m_prev - m_new)
                                    + _dot(p.astype(BF16), v_ref[hd, cols, :]))
            m_ref[hd, rows, :] = m_new

    def head_pairs(diagonal):
        def body(pair, carry):
            for ks in range(nsub):
                for qs in range(ks if diagonal else 0, nsub):
                    unit(pair, qs, ks, diagonal and ks == qs)
            return carry
        lax.fori_loop(0, n_heads // 2, body, 0)

    @pl.when(ki < qi)
    def _():
        head_pairs(False)

    @pl.when(ki == qi)
    def _():
        head_pairs(True)
        for pair in range(n_heads // 2):
            outs = []
            for hd in (2 * pair, 2 * pair + 1):
                a = acc_ref[hd]
                outs.append((a * (1.0 / a[:, V_HEAD:V_HEAD + 1]))[:, :V_HEAD])
            o_ref[:, pair * LANES:(pair + 1) * LANES] = jnp.concatenate(outs, axis=1).astype(o_ref.dtype)


def _flash_attend(q2, k_all, v_all, batch, t):
    tq = min(ATT_TILE, t)
    nq = t // tq
    pairs = [(qi, ki) for qi in range(nq) for ki in range(qi + 1)]
    qt = jnp.asarray([p[0] for p in pairs], jnp.int32)
    kt = jnp.asarray([p[1] for p in pairs], jnp.int32)
    hps = FLASH_HEADS_PER_STEP
    grid_spec = pltpu.PrefetchScalarGridSpec(
        num_scalar_prefetch=2,
        grid=(batch, MLA_HEADS // hps, len(pairs)),
        in_specs=[pl.BlockSpec((hps, tq, LANES), lambda b, hg, s, qt, kt: (hg, b * nq + qt[s], 0)),
                  pl.BlockSpec((hps * LANES, tq), lambda b, hg, s, qt, kt: (hg, b * nq + kt[s])),
                  pl.BlockSpec((hps, tq, LANES), lambda b, hg, s, qt, kt: (hg, b * nq + kt[s], 0))],
        out_specs=pl.BlockSpec((tq, hps * V_HEAD), lambda b, hg, s, qt, kt: (b * nq + qt[s], hg)),
        scratch_shapes=[pltpu.VMEM((hps, tq, LANES), F32), pltpu.VMEM((hps, tq, LANES), F32)],
    )
    return pl.pallas_call(
        functools.partial(_flash_kernel, sub=min(ATT_SUB, tq)),
        out_shape=jax.ShapeDtypeStruct((batch * t, MLA_HEADS * V_HEAD), BF16),
        grid_spec=grid_spec,
        compiler_params=_cparams(("parallel", "parallel", "arbitrary")),
        name="mla_flash",
    )(qt, kt, q2, k_all, v_all)


def _decode_kernel(pt_ref, q_ref, cn_ref, pn_ref, ckv_hbm, kpe_hbm, o_ref, ckv_buf, kpe_buf, sem, kv_ref, s_ref,
                   *, n_pages, page, chunk):
    b = pl.program_id(0)
    last = pl.num_programs(0) - 1
    slot = b % 2
    rows = q_ref.shape[1]
    n_chunks = n_pages * page // chunk
    pages_per_chunk = n_pages // n_chunks

    def page_copies(pid, sl, p):
        rows_p = pl.ds(p * page, page)
        return (pltpu.make_async_copy(ckv_hbm.at[pid], ckv_buf.at[sl, rows_p, :], sem.at[0, sl]),
                pltpu.make_async_copy(kpe_hbm.at[pid], kpe_buf.at[sl, :, rows_p], sem.at[1, sl]))

    def start_pages(bb, sl, p0, p1):
        for p in range(p0, p1):
            for cp in page_copies(pt_ref[bb * n_pages + p], sl, p):
                cp.start()

    def wait_pages(sl):
        for p in range(n_pages):
            for cp in page_copies(0, sl, p):
                cp.wait()

    @pl.when(b == 0)
    def _():
        start_pages(0, 0, 0, n_pages)

    wait_pages(slot)
    nxt = jnp.minimum(b + 1, last)

    q = q_ref[0]
    ql = q[:, :KV_LORA]
    qp = q[:, KV_LORA:KV_LORA + QK_ROPE]
    past = n_pages * page

    for c in range(n_chunks):
        start_pages(nxt, 1 - slot, c * pages_per_chunk, (c + 1) * pages_per_chunk)
        cs = slice(c * chunk, (c + 1) * chunk)
        kc = ckv_buf[slot, cs, :].astype(BF16)
        kp = kpe_buf[slot, :, cs].astype(BF16)
        kv_ref[cs, :] = kc
        s_ref[:, cs] = _dot_nt(ql, kc) + _dot(qp, kp)

    t_new = cn_ref.shape[1]
    pad = jnp.zeros((LANES - t_new, KV_LORA), F32)
    kc = jnp.concatenate([cn_ref[0], pad], axis=0).astype(BF16)
    kp = jnp.concatenate([pn_ref[0], pad[:, :LANES]], axis=0).astype(BF16)
    s = _dot_nt(ql, kc) + _dot_nt(q[:, KV_LORA:], kp)
    r = lax.broadcasted_iota(jnp.int32, (rows, LANES), 0)
    c = lax.broadcasted_iota(jnp.int32, (rows, LANES), 1)
    kv_ref[past:, :] = kc
    s_ref[:, past:] = jnp.where(c <= r // MLA_HEADS, s, NEG)

    s_all = s_ref[...]
    p = jnp.exp2(s_all - jnp.max(s_all, axis=-1, keepdims=True))
    l = jnp.sum(p, axis=-1, keepdims=True)
    o_ref[0] = (_dot(p.astype(BF16), kv_ref[...]) / l).astype(o_ref.dtype)

    @pl.when(b == last)
    def _():
        wait_pages(1 - slot)


def _decode_attend(q_abs, ckv_new, kpe_new, cache_ckv, cache_kpe_t, page_table):
    batch, rows, _ = q_abs.shape
    n_pages = page_table.shape[1]
    page = cache_ckv.shape[1]
    past = n_pages * page
    t_new = ckv_new.shape[1]
    per_b = lambda b, pt: (b, 0, 0)
    grid_spec = pltpu.PrefetchScalarGridSpec(
        num_scalar_prefetch=1,
        grid=(batch,),
        in_specs=[pl.BlockSpec((1, rows, LAT_W), per_b),
                  pl.BlockSpec((1, t_new, KV_LORA), per_b),
                  pl.BlockSpec((1, t_new, LANES), per_b),
                  pl.BlockSpec(memory_space=pl.ANY),
                  pl.BlockSpec(memory_space=pl.ANY)],
        out_specs=pl.BlockSpec((1, rows, KV_LORA), per_b),
        scratch_shapes=[pltpu.VMEM((2, past, KV_LORA), F32),
                        pltpu.VMEM((2, QK_ROPE, past), F32),
                        pltpu.SemaphoreType.DMA((2, 2)),
                        pltpu.VMEM((past + LANES, KV_LORA), BF16),
                        pltpu.VMEM((rows, past + LANES), F32)],
    )
    return pl.pallas_call(
        functools.partial(_decode_kernel, n_pages=n_pages, page=page, chunk=min(DECODE_CHUNK, past)),
        out_shape=jax.ShapeDtypeStruct((batch, rows, KV_LORA), BF16),
        grid_spec=grid_spec,
        compiler_params=_cparams(("arbitrary",)),
        name="mla_decode",
    )(page_table.reshape(-1), q_abs, ckv_new, kpe_new, cache_ckv, cache_kpe_t)


def _pad_lanes(v, width):
    return jnp.pad(v, [(0, 0)] * (v.ndim - 1) + [(0, width - v.shape[-1])])


def _ssd_params(i, w_in_a, conv_w, conv_b, dt_bias, a_log, d_skip, g_ssd_norm):
    w = w_in_a[i]
    z0, x0, d0, m0 = 0, D_INNER, D_INNER + CONV_DIM, D_INNER + CONV_DIM + SSD_HEADS
    w_main = jnp.concatenate([w[:, x0:d0], w[:, m0:], w[:, z0:x0]], axis=1).astype(BF16)
    w_dt = _pad_lanes(w[:, d0:m0], LANES).astype(BF16)
    head_of = jnp.arange(D_INNER, dtype=jnp.int32) // SSD_HEAD_DIM
    expand = (jnp.arange(LANES, dtype=jnp.int32)[:, None] == head_of[None, :]).astype(BF16)
    return dict(w_main=w_main, w_dt=w_dt, conv_w=conv_w[i], conv_b=conv_b[i].reshape(1, CONV_DIM),
                dt_bias=_pad_lanes(dt_bias[i].reshape(1, SSD_HEADS), LANES),
                a_log=_pad_lanes(a_log[i].reshape(1, SSD_HEADS), LANES),
                d_x=jnp.repeat(d_skip[i], SSD_HEAD_DIM).reshape(1, D_INNER),
                g_norm=g_ssd_norm[i].reshape(1, D_INNER), expand=expand)


def _mla_shared_params(w_kv_a, w_uk, w_uv):
    half = QK_ROPE // 2
    wa = _pad_lanes(w_kv_a, LAT_W).astype(BF16)
    pe = w_kv_a[:, KV_LORA:]
    wr = _pad_lanes(jnp.concatenate([-pe[:, half:], pe[:, :half]], axis=1), LANES).astype(BF16)
    eye = jnp.eye(QK_ROPE, dtype=F32)
    k_blocks, abs_blocks = [], []
    for h in range(MLA_HEADS):
        kb = jnp.zeros((LAT_W, LANES), F32)
        kb = kb.at[:KV_LORA, :QK_NOPE].set(w_uk[:, h, :])
        kb = kb.at[KV_LORA:KV_LORA + QK_ROPE, QK_NOPE:QK_NOPE + QK_ROPE].set(eye)
        k_blocks.append(kb)
        ab = jnp.zeros((LANES, LAT_W), F32)
        ab = ab.at[:QK_NOPE, :KV_LORA].set(w_uk[:, h, :].T)
        ab = ab.at[QK_NOPE:QK_NOPE + QK_ROPE, KV_LORA:KV_LORA + QK_ROPE].set(eye)
        abs_blocks.append(ab)
    w_k = jnp.concatenate(k_blocks, axis=1).T.astype(BF16)
    v_blocks = []
    for h in range(MLA_HEADS):
        vb = jnp.zeros((LAT_W, LANES), F32).at[:KV_LORA, :V_HEAD].set(w_uv[:, h, :])
        v_blocks.append(vb.at[KV_LORA + QK_ROPE, V_HEAD].set(1.0))
    w_v = jnp.concatenate(v_blocks, axis=1).astype(BF16)
    w_abs = jnp.stack(abs_blocks).astype(BF16)
    zero = jnp.zeros((KV_LORA, V_HEAD), F32)
    w_uvp = jnp.stack([jnp.block([[w_uv[:, 2 * j, :], zero], [zero, w_uv[:, 2 * j + 1, :]]])
                       for j in range(MLA_HEADS // 2)]).astype(BF16)
    return dict(wa=wa, wr=wr, w_k=w_k, w_v=w_v, w_abs=w_abs, w_uvp=w_uvp)


def _mla_layer_params(i, w_in_b, w_q_b):
    half = QK_ROPE // 2
    w = w_in_b[i]
    w_inb = jnp.concatenate([w[:, Q_LORA:], w[:, :Q_LORA]], axis=1).astype(BF16)
    wq = w_q_b[i]
    wqa = _pad_lanes(wq, LANES).reshape(Q_LORA, MLA_HEADS * LANES).astype(BF16)
    rot = jnp.concatenate([jnp.zeros_like(wq[..., :QK_NOPE]), -wq[..., QK_NOPE + half:], wq[..., QK_NOPE:QK_NOPE + half]],
                          axis=-1)
    wqb = _pad_lanes(rot, LANES).reshape(Q_LORA, MLA_HEADS * LANES).astype(BF16)
    return dict(w_inb=w_inb, wqa=wqa, wqb=wqb)


def _rope_tables(pos):
    half = QK_ROPE // 2
    inv = ROPE_BASE ** (-jnp.arange(half, dtype=F32) / half)
    ang = pos.astype(F32)[:, None] * inv[None, :]
    cos, sin = jnp.cos(ang), jnp.sin(ang)
    n = pos.shape[0]
    one = lambda w: jnp.ones((n, w), F32)
    zero = lambda w: jnp.zeros((n, w), F32)
    tail = LANES - QK_NOPE - QK_ROPE
    cos_q = jnp.concatenate([one(QK_NOPE), cos, cos, one(tail)], axis=1)
    sin_q = jnp.concatenate([zero(QK_NOPE), sin, sin, zero(tail)], axis=1)
    cos_k = jnp.concatenate([cos, cos, one(LANES - QK_ROPE)], axis=1)
    sin_k = jnp.concatenate([sin, sin, zero(LANES - QK_ROPE)], axis=1)
    return cos_q, sin_q, cos_k, sin_k


def _trunk(x, batch, t, sample, mem_attend, ssd_state, mla_cache, tabs, P):
    act = F32 if sample else BF16
    cos_q, sin_q, cos_k, sin_k = tabs
    ssm_out, conv_out, stacked = [], [], None
    ckv = kpe = None
    for l in range(DEPTH):
        if l < N_A:
            sp = P["ssd"][l]
            proj = _proj(x, sp["w_main"], g=P["g_pre_mix"][l], out_dtype=act, tn=1024, row_tiles=2, name=f"in_a{l}")
            dt = _proj(x, sp["w_dt"], g=P["g_pre_mix"][l], out_dtype=F32, name=f"in_dt{l}")
            if sample:
                state_ssm, state_conv = ssd_state
                y, stacked = _ssd_sample(proj, dt, state_conv, state_ssm.reshape(N_A, batch, D_INNER, D_STATE),
                                         stacked, l, sp, batch, t)
            else:
                y, h_new = _ssd_prompt(proj, dt, sp, batch, t)
                ssm_out.append(h_new)
                conv_out.append(proj.reshape(batch, t, -1)[:, t - 3:, :CONV_DIM].astype(F32))
            mo = mem_attend(l, proj, 3)
            w_out = P["w_out_a"][l]
            x = _outproj(y, mo, w_out[:D_INNER], w_out[D_INNER:], P["g_post_mix"][l], x, f"out_a{l}")
        else:
            i = l - N_A
            ms, ml = P["mla_shared"], P["mla"][i]
            if i == 0:
                ckv, kpe, lat = _kv_latent(x, P["g_kv_in"], ms["wa"], ms["wr"], P["g_kv_norm"], cos_k, sin_k)
                if not sample:
                    k_all = _proj_t(lat, ms["w_k"], "k_heads")
                    v_all = _proj(lat, ms["w_v"], tn=1024, head_major=True, name="v_heads")
            proj = _proj(x, ml["w_inb"], g=P["g_pre_mix"][l], out_dtype=act, name=f"in_b{i}")
            q2 = _proj(proj, ml["wqa"], g=P["g_q_norm"][i], xcol=2, tn=1024,
                       rope=(ml["wqb"], cos_q, sin_q, MLA_SCALE * LOG2E), head_major=not sample,
                       name=f"q_heads{i}")
            if sample:
                cache_ckv, cache_kpe, page_table = mla_cache
                q_abs = _block_proj(q2, ms["w_abs"], f"q_abs{i}").reshape(batch, t * MLA_HEADS, LAT_W)
                o_lat = _decode_attend(q_abs, ckv.reshape(batch, t, KV_LORA), kpe.reshape(batch, t, LANES),
                                       cache_ckv, cache_kpe, page_table)
                o = _block_proj(o_lat.reshape(batch * t, MLA_HEADS * KV_LORA), ms["w_uvp"], f"o_heads{i}")
            else:
                o = _flash_attend(q2, k_all, v_all, batch, t)
            mo = mem_attend(l, proj, 0)
            w_out = P["w_out_b"][i]
            n_o = MLA_HEADS * V_HEAD
            x = _outproj(o, mo, w_out[:n_o], w_out[n_o:], P["g_post_mix"][l], x, f"out_b{i}")
        x = _ffn(x, P["g_pre_ffn"][l], P["w_ffn_up"][l], P["w_ffn_down"][l], P["g_post_ffn"][l], f"ffn{l}")
    conv_new, ssm_new = stacked if sample else (jnp.stack(conv_out), jnp.stack(ssm_out))
    ssm_new = ssm_new.reshape(N_A, batch, SSD_HEADS, SSD_HEAD_DIM, D_STATE)
    return x, ssm_new, conv_new, ckv, kpe[:, :QK_ROPE]


def kernel(x_prompt, x_sample, mem_prompt, state_ssm, state_conv, cache_ckv, cache_kpe, cache_mem_k, cache_mem_v, page_table, g_pre_mix, g_post_mix, g_pre_ffn, g_post_ffn, w_ffn_up, w_ffn_down, w_mem_k, w_mem_v, w_in_a, conv_w, conv_b, dt_bias, a_log, d_skip, g_ssd_norm, w_out_a, g_kv_in, w_kv_a, g_kv_norm, w_uk, w_uv, w_in_b, g_q_norm, w_q_b, w_out_b):
    b, s, _ = x_prompt.shape
    db, t, _ = x_sample.shape
    past_len = page_table.shape[1] * cache_ckv.shape[1]

    P = dict(
        g_pre_mix=g_pre_mix, g_post_mix=g_post_mix, g_pre_ffn=g_pre_ffn, g_post_ffn=g_post_ffn,
        g_kv_in=g_kv_in, g_kv_norm=g_kv_norm, g_q_norm=g_q_norm,
        w_ffn_up=w_ffn_up.astype(BF16), w_ffn_down=w_ffn_down.astype(BF16),
        w_out_a=w_out_a.astype(BF16), w_out_b=w_out_b.astype(BF16),
        ssd=[_ssd_params(i, w_in_a, conv_w, conv_b, dt_bias, a_log, d_skip, g_ssd_norm) for i in range(N_A)],
        mla_shared=_mla_shared_params(w_kv_a, w_uk, w_uv),
        mla=[_mla_layer_params(i, w_in_b, w_q_b) for i in range(DEPTH - N_A)],
    )

    w_mem = jnp.concatenate([w_mem_k[l] for l in range(DEPTH)] + [w_mem_v[l] for l in range(DEPTH)], axis=1)
    mem_kv = _proj(mem_prompt.reshape(b * N_MEM, D_MODEL), w_mem.astype(BF16), out_dtype=F32, tn=1024, name="mem_kv")
    mem_kv5 = mem_kv.reshape(b, N_MEM, 2 * DEPTH, MEM_HEADS, MEM_HEAD_DIM)
    p_mem_k = jnp.transpose(mem_kv5[:, :, :DEPTH], (2, 0, 1, 3, 4))
    p_mem_v = jnp.transpose(mem_kv5[:, :, DEPTH:], (2, 0, 1, 3, 4))
    mem_kv3 = mem_kv.reshape(b, N_MEM, 2 * DEPTH * MEM_WIDTH)
    tabs_p = _rope_tables(jnp.arange(s, dtype=jnp.int32))
    y_prompt, p_ssm, p_conv, p_ckv, p_kpe = _trunk(
        x_prompt.reshape(b * s, D_MODEL), b, s, False,
        lambda l, q, qcol: _mem_attend(q, qcol, mem_kv3, mem_kv3, 0, l, DEPTH + l, b, s, BF16, f"mem{l}"),
        None, None, tabs_p, P)

    tabs_s = tuple(jnp.tile(tb, (db, 1)) for tb in _rope_tables(past_len + jnp.arange(t, dtype=jnp.int32)))
    cache_kpe_t = jnp.swapaxes(cache_kpe, 1, 2)
    y_sample, s_ssm, s_conv, s_ckv, s_kpe = _trunk(
        x_sample.reshape(db * t, D_MODEL), db, t, True,
        lambda l, q, qcol: _mem_attend_cache(q, qcol, cache_mem_k, cache_mem_v, l, db, t, f"mem_s{l}"),
        (state_ssm, state_conv), (cache_ckv, cache_kpe_t, page_table), tabs_s, P)

    return (y_prompt.reshape(b, s, D_MODEL), y_sample.reshape(db, t, D_MODEL),
            p_ssm, p_conv, p_ckv.reshape(b, s, KV_LORA), p_kpe.reshape(b, s, QK_ROPE), p_mem_k, p_mem_v,
            s_ssm, s_conv, s_ckv.reshape(db, t, KV_LORA), s_kpe.reshape(db, t, QK_ROPE))
```

```python
import functools

import jax
import jax.numpy as jnp
from jax import lax
from jax.experimental import pallas as pl
from jax.experimental.pallas import tpu as pltpu

F32 = jnp.float32
BF16 = jnp.bfloat16

D_MODEL = 1024
DEPTH = 4
N_A = 2
D_INNER = 2048
SSD_HEADS = 32
SSD_HEAD_DIM = 64
SSD_GROUPS = 4
GROUP_W = D_INNER // SSD_GROUPS
D_STATE = 128
CONV_DIM = 3072
SSD_CHUNK = 128
MLA_HEADS = 16
Q_LORA = 512
KV_LORA = 256
QK_NOPE = 64
QK_ROPE = 32
V_HEAD = 64
ROPE_BASE = 10000.0
MLA_SCALE = (QK_NOPE + QK_ROPE) ** -0.5
LOG2E = 1.4426950408889634
N_MEM = 256
MEM_HEADS = 4
MEM_HEAD_DIM = 256
MEM_WIDTH = 1024
D_FF = 4096
RMS_EPS = 1e-6
LANES = 128
LAT_W = 384
NEG = -1e30
ROW_TILE = 512
ATT_TILE = 1024
ATT_SUB = 512
FLASH_HEADS_PER_STEP = 8
DECODE_CHUNK = 1024
MEM_BATCH_BLOCK = 4

NT_DIMS = (((1,), (1,)), ((), ()))
TN_DIMS = (((0,), (0,)), ((), ()))


def _cparams(sem, vmem_mb=48):
    return pltpu.CompilerParams(dimension_semantics=sem, vmem_limit_bytes=vmem_mb * 1024 * 1024)


def _rms(x, g):
    ms = jnp.mean(x * x, axis=-1, keepdims=True)
    return x * lax.rsqrt(ms + RMS_EPS) * g


def _silu(x):
    return x * (1.0 / (1.0 + jnp.exp(-x)))


def _softplus(x):
    return jnp.maximum(x, 0.0) + jnp.log1p(jnp.exp(-jnp.abs(x)))


def _dot(a, b):
    return jnp.dot(a, b, preferred_element_type=F32)


def _dot_nt(a, b):
    return lax.dot_general(a, b, NT_DIMS, preferred_element_type=F32)


def _dot_tn(a, b):
    return lax.dot_general(a, b, TN_DIMS, preferred_element_type=F32)


def _split_bf16(v, terms):
    parts = []
    for _ in range(terms):
        p = v.astype(BF16)
        parts.append(p)
        v = v - p.astype(F32)
    return parts


def _sel_dot(m, v, terms=2):
    return sum(_dot(m, p) for p in _split_bf16(v, terms))


def _expand(v, e_ref, terms=2):
    return sum(_dot(p, e_ref[...]) for p in _split_bf16(v, terms))


def _proj_kernel(*refs, has_g, has_rope, scale):
    it = iter(refs)
    x_ref = next(it)
    g_ref = next(it) if has_g else None
    w_ref = next(it)
    if has_rope:
        w2_ref, cos_ref, sin_ref = next(it), next(it), next(it)
    o_ref = next(it)
    xn_ref = next(it)

    @pl.when(pl.program_id(1) == 0)
    def _():
        x = x_ref[...].astype(F32)
        if has_g:
            x = _rms(x, g_ref[...])
        xn_ref[...] = x.astype(BF16)

    xn = xn_ref[...]
    y = _dot(xn, w_ref[...])
    if has_rope:
        rep = y.shape[1] // LANES
        cos = jnp.concatenate([cos_ref[...]] * rep, axis=1)
        sin = jnp.concatenate([sin_ref[...]] * rep, axis=1)
        y = (y * cos + _dot(xn, w2_ref[...]) * sin) * scale
    if len(o_ref.shape) == 3:
        for hh in range(o_ref.shape[0]):
            o_ref[hh] = y[:, hh * LANES:(hh + 1) * LANES].astype(o_ref.dtype)
    else:
        o_ref[...] = y.astype(o_ref.dtype)


def _proj(x, w, *, g=None, xcol=0, out_dtype=BF16, tn=None, rope=None, row_tiles=1, head_major=False, name):
    n = x.shape[0]
    k, nout = w.shape
    tm = min(ROW_TILE * row_tiles, n)
    tn = nout if tn is None else tn
    if head_major:
        out_shape = jax.ShapeDtypeStruct((nout // LANES, n, LANES), out_dtype)
        out_spec = pl.BlockSpec((tn // LANES, tm, LANES), lambda i, j: (j, i, 0))
    else:
        out_shape = jax.ShapeDtypeStruct((n, nout), out_dtype)
        out_spec = pl.BlockSpec((tm, tn), lambda i, j: (i, j))
    in_specs = [pl.BlockSpec((tm, k), lambda i, j: (i, xcol))]
    args = [x]
    if g is not None:
        in_specs.append(pl.BlockSpec((1, k), lambda i, j: (0, 0)))
        args.append(g.reshape(1, k).astype(F32))
    in_specs.append(pl.BlockSpec((k, tn), lambda i, j: (0, j)))
    args.append(w)
    scale = 1.0
    if rope is not None:
        w2, cos, sin, scale = rope
        nb = cos.shape[0] // tm
        in_specs.append(pl.BlockSpec((k, tn), lambda i, j: (0, j)))
        in_specs.append(pl.BlockSpec((tm, LANES), lambda i, j: (i % nb, 0)))
        in_specs.append(pl.BlockSpec((tm, LANES), lambda i, j: (i % nb, 0)))
        args += [w2, cos, sin]
    return pl.pallas_call(
        functools.partial(_proj_kernel, has_g=g is not None, has_rope=rope is not None, scale=scale),
        out_shape=out_shape,
        grid=(n // tm, nout // tn),
        in_specs=in_specs,
        out_specs=out_spec,
        scratch_shapes=[pltpu.VMEM((tm, k), BF16)],
        compiler_params=_cparams(("parallel", "arbitrary")),
        name=name,
    )(*args)


def _proj_t_kernel(x_ref, w_ref, o_ref):
    o_ref[...] = _dot_nt(w_ref[...], x_ref[...]).astype(o_ref.dtype)


def _proj_t(x, w_t, name):
    n, k = x.shape
    nout = w_t.shape[0]
    tm = min(ROW_TILE, n)
    return pl.pallas_call(
        _proj_t_kernel,
        out_shape=jax.ShapeDtypeStruct((nout, n), BF16),
        grid=(n // tm,),
        in_specs=[pl.BlockSpec((tm, k), lambda i: (i, 0)), pl.BlockSpec((nout, k), lambda i: (0, 0))],
        out_specs=pl.BlockSpec((nout, tm), lambda i: (0, i)),
        compiler_params=_cparams(("parallel",)),
        name=name,
    )(x, w_t)


def _block_proj_kernel(x_ref, w_ref, o_ref):
    o_ref[...] = _dot(x_ref[...].astype(BF16), w_ref[0]).astype(o_ref.dtype)


def _block_proj(x, w, name):
    n = x.shape[0]
    nb, kx, ko = w.shape
    tm = min(ROW_TILE, n)
    return pl.pallas_call(
        _block_proj_kernel,
        out_shape=jax.ShapeDtypeStruct((n, nb * ko), BF16),
        grid=(n // tm, nb),
        in_specs=[pl.BlockSpec((tm, kx), lambda i, j: (i, j)), pl.BlockSpec((1, kx, ko), lambda i, j: (j, 0, 0))],
        out_specs=pl.BlockSpec((tm, ko), lambda i, j: (i, j)),
        compiler_params=_cparams(("parallel", "parallel")),
        name=name,
    )(x, w)


def _kvlat_kernel(x_ref, g_ref, wa_ref, wr_ref, gkv_ref, cos_ref, sin_ref, ckv_ref, kpe_ref, lat_ref):
    xn = _rms(x_ref[...], g_ref[...]).astype(BF16)
    a = _dot(xn, wa_ref[...])
    r = _dot(xn, wr_ref[...])
    ckv = _rms(a[:, :KV_LORA], gkv_ref[...])
    pe = a[:, KV_LORA:] * cos_ref[...] + r * sin_ref[...]
    ckv_ref[...] = ckv
    kpe_ref[...] = pe
    lat_ref[:, :KV_LORA] = ckv.astype(BF16)
    lane = lax.broadcasted_iota(jnp.int32, pe.shape, 1)
    lat_ref[:, KV_LORA:] = jnp.where(lane == QK_ROPE, 1.0, pe).astype(BF16)


def _kv_latent(x, g, wa, wr, gkv, cos, sin):
    n = x.shape[0]
    tm = min(ROW_TILE, n)
    nb = cos.shape[0] // tm
    row = lambda i: (i, 0)
    const = lambda i: (0, 0)
    return pl.pallas_call(
        _kvlat_kernel,
        out_shape=(jax.ShapeDtypeStruct((n, KV_LORA), F32),
                   jax.ShapeDtypeStruct((n, LANES), F32),
                   jax.ShapeDtypeStruct((n, LAT_W), BF16)),
        grid=(n // tm,),
        in_specs=[pl.BlockSpec((tm, D_MODEL), row),
                  pl.BlockSpec((1, D_MODEL), const),
                  pl.BlockSpec((D_MODEL, LAT_W), const),
                  pl.BlockSpec((D_MODEL, LANES), const),
                  pl.BlockSpec((1, KV_LORA), const),
                  pl.BlockSpec((tm, LANES), lambda i: (i % nb, 0)),
                  pl.BlockSpec((tm, LANES), lambda i: (i % nb, 0))],
        out_specs=(pl.BlockSpec((tm, KV_LORA), row),
                   pl.BlockSpec((tm, LANES), row),
                   pl.BlockSpec((tm, LAT_W), row)),
        compiler_params=_cparams(("parallel",)),
        name="kv_latent",
    )(x, g.reshape(1, D_MODEL), wa, wr, gkv.reshape(1, KV_LORA), cos, sin)


def _outproj_kernel(a1_ref, a2_ref, w1_ref, w2_ref, g_ref, r_ref, o_ref):
    acc = _dot(a1_ref[...].astype(BF16), w1_ref[...]) + _dot(a2_ref[...].astype(BF16), w2_ref[...])
    o_ref[...] = r_ref[...] + _rms(acc, g_ref[...])


def _outproj(a1, a2, w1, w2, g, res, name):
    n = res.shape[0]
    tm = min(ROW_TILE, n)
    k1, k2 = w1.shape[0], w2.shape[0]
    row = lambda i: (i, 0)
    const = lambda i: (0, 0)
    return pl.pallas_call(
        _outproj_kernel,
        out_shape=jax.ShapeDtypeStruct((n, D_MODEL), F32),
        grid=(n // tm,),
        in_specs=[pl.BlockSpec((tm, k1), row), pl.BlockSpec((tm, k2), row),
                  pl.BlockSpec((k1, D_MODEL), const), pl.BlockSpec((k2, D_MODEL), const),
                  pl.BlockSpec((1, D_MODEL), const), pl.BlockSpec((tm, D_MODEL), row)],
        out_specs=pl.BlockSpec((tm, D_MODEL), row),
        compiler_params=_cparams(("parallel",)),
        name=name,
    )(a1, a2, w1, w2, g.reshape(1, D_MODEL), res)


def _ffn_kernel(x_ref, g1_ref, wu_ref, wd_ref, g2_ref, o_ref):
    x = x_ref[...]
    xn = _rms(x, g1_ref[...]).astype(BF16)
    acc = jnp.zeros(x.shape, F32)
    for c in range(D_FF // D_MODEL):
        sl = slice(c * D_MODEL, (c + 1) * D_MODEL)
        h = _dot(xn, wu_ref[:, sl])
        h = jnp.square(jnp.maximum(h, 0.0)).astype(BF16)
        acc = acc + _dot(h, wd_ref[sl, :])
    o_ref[...] = x + _rms(acc, g2_ref[...])


def _ffn(x, g1, wu, wd, g2, name):
    n = x.shape[0]
    tm = min(ROW_TILE, n)
    row = lambda i: (i, 0)
    const = lambda i: (0, 0)
    return pl.pallas_call(
        _ffn_kernel,
        out_shape=jax.ShapeDtypeStruct((n, D_MODEL), F32),
        grid=(n // tm,),
        in_specs=[pl.BlockSpec((tm, D_MODEL), row), pl.BlockSpec((1, D_MODEL), const),
                  pl.BlockSpec((D_MODEL, D_FF), const, pipeline_mode=pl.Buffered(1)),
                  pl.BlockSpec((D_FF, D_MODEL), const, pipeline_mode=pl.Buffered(1)),
                  pl.BlockSpec((1, D_MODEL), const)],
        out_specs=pl.BlockSpec((tm, D_MODEL), row),
        compiler_params=_cparams(("parallel",), vmem_mb=48),
        name=name,
    )(x, g1.reshape(1, D_MODEL), wu, wd, g2.reshape(1, D_MODEL))


def _mem_head(qh, kh, vh):
    s = _dot_nt(qh.astype(BF16), kh.astype(BF16)) * (MEM_HEAD_DIM ** -0.5)
    m = jnp.max(s, axis=-1, keepdims=True)
    p = jnp.exp(s - m)
    l = jnp.sum(p, axis=-1, keepdims=True)
    return _dot(p.astype(BF16), vh.astype(BF16)) / l


def _mem_kernel(q_ref, k_ref, v_ref, o_ref):
    q = q_ref[...]
    for h in range(MEM_HEADS):
        sl = slice(h * MEM_HEAD_DIM, (h + 1) * MEM_HEAD_DIM)
        o_ref[:, sl] = _mem_head(q[:, sl], k_ref[0, :, sl], v_ref[0, :, sl]).astype(o_ref.dtype)


def _mem_cache_kernel(q_ref, k_hbm, v_hbm, o_ref, kbuf, vbuf, sem, *, layer, bb, t):
    step = pl.program_id(0)
    slot = step % 2

    def copies(st, sl):
        out = []
        for i in range(bb):
            for h in range(MEM_HEADS):
                out.append(pltpu.make_async_copy(k_hbm.at[layer, st * bb + i, :, h, :], kbuf.at[sl, i, h], sem.at[0, sl]))
                out.append(pltpu.make_async_copy(v_hbm.at[layer, st * bb + i, :, h, :], vbuf.at[sl, i, h], sem.at[1, sl]))
        return out

    @pl.when(step == 0)
    def _():
        for cp in copies(0, 0):
            cp.start()

    @pl.when(step + 1 < pl.num_programs(0))
    def _():
        for cp in copies(step + 1, 1 - slot):
            cp.start()

    for cp in copies(step, slot):
        cp.wait()

    r = lax.broadcasted_iota(jnp.int32, (MEM_HEADS * t, MEM_HEADS * N_MEM), 0)
    c = lax.broadcasted_iota(jnp.int32, (MEM_HEADS * t, MEM_HEADS * N_MEM), 1)
    same_head = (r // t) == (c // N_MEM)
    for i in range(bb):
        rows = slice(i * t, (i + 1) * t)
        qs = jnp.concatenate([q_ref[rows, h * MEM_HEAD_DIM:(h + 1) * MEM_HEAD_DIM] for h in range(MEM_HEADS)], axis=0)
        ks = jnp.concatenate([kbuf[slot, i, h].astype(BF16) for h in range(MEM_HEADS)], axis=0)
        vs = jnp.concatenate([vbuf[slot, i, h].astype(BF16) for h in range(MEM_HEADS)], axis=0)
        s = jnp.where(same_head, _dot_nt(qs.astype(BF16), ks) * (MEM_HEAD_DIM ** -0.5), NEG)
        p = jnp.exp(s - jnp.max(s, axis=-1, keepdims=True))
        o = _dot(p.astype(BF16), vs) / jnp.sum(p, axis=-1, keepdims=True)
        for h in range(MEM_HEADS):
            o_ref[rows, h * MEM_HEAD_DIM:(h + 1) * MEM_HEAD_DIM] = o[h * t:(h + 1) * t, :].astype(o_ref.dtype)


def _mem_attend_cache(q, qcol, cache_k, cache_v, layer, batch, t, name):
    bb = min(MEM_BATCH_BLOCK, batch)
    buf = pltpu.VMEM((2, bb, MEM_HEADS, N_MEM, MEM_HEAD_DIM), F32)
    return pl.pallas_call(
        functools.partial(_mem_cache_kernel, layer=layer, bb=bb, t=t),
        out_shape=jax.ShapeDtypeStruct((batch * t, MEM_WIDTH), F32),
        grid=(batch // bb,),
        in_specs=[pl.BlockSpec((bb * t, MEM_WIDTH), lambda i: (i, qcol)),
                  pl.BlockSpec(memory_space=pl.ANY), pl.BlockSpec(memory_space=pl.ANY)],
        out_specs=pl.BlockSpec((bb * t, MEM_WIDTH), lambda i: (i, 0)),
        scratch_shapes=[buf, buf, pltpu.SemaphoreType.DMA((2, 2))],
        compiler_params=_cparams(("arbitrary",)),
        name=name,
    )(q, cache_k, cache_v)


def _mem_attend(q, qcol, k_arr, v_arr, kboff, kcol, vcol, batch, t, out_dtype, name):
    tq = min(ROW_TILE, t)
    nt = t // tq
    return pl.pallas_call(
        _mem_kernel,
        out_shape=jax.ShapeDtypeStruct((batch * t, MEM_WIDTH), out_dtype),
        grid=(batch, nt),
        in_specs=[pl.BlockSpec((tq, MEM_WIDTH), lambda b, i: (b * nt + i, qcol)),
                  pl.BlockSpec((1, N_MEM, MEM_WIDTH), lambda b, i: (b + kboff, 0, kcol)),
                  pl.BlockSpec((1, N_MEM, MEM_WIDTH), lambda b, i: (b + kboff, 0, vcol))],
        out_specs=pl.BlockSpec((tq, MEM_WIDTH), lambda b, i: (b * nt + i, 0)),
        compiler_params=_cparams(("parallel", "parallel")),
        name=name,
    )(q, k_arr, v_arr)


def _gate_norm_store(y_ref, ygrp, z_ref, gn_ref, g):
    sl = slice(g * GROUP_W, (g + 1) * GROUP_W)
    ygrp = ygrp * _silu(z_ref[:, sl].astype(F32))
    y_ref[:, sl] = _rms(ygrp, gn_ref[:, sl]).astype(y_ref.dtype)


def _ssd_prompt_kernel(xs_ref, bc_ref, z_ref, dt_ref, cw_ref, cb_ref, dtb_ref, alog_ref, d_ref, gn_ref, e_ref, sh_ref,
                       y_ref, hout_ref, extx, extbc, ht):
    q = SSD_CHUNK
    c = pl.program_id(1)
    tail = 16

    @pl.when(c == 0)
    def _():
        extx[0:q] = jnp.zeros((q, D_INNER), BF16)
        extbc[0:q] = jnp.zeros((q, CONV_DIM - D_INNER), BF16)
        ht[...] = jnp.zeros(ht.shape, F32)

    @pl.when(c > 0)
    def _():
        extx[q - tail:q] = extx[2 * q - tail:2 * q]
        extbc[q - tail:q] = extbc[2 * q - tail:2 * q]

    extx[q:2 * q] = xs_ref[...]
    extbc[q:2 * q] = bc_ref[...]

    def conv(ext, w, b):
        sh = _dot(sh_ref[...], ext[...])
        acc = (b + w[3:4] * ext[q:2 * q].astype(F32) + w[2:3] * sh[0:q] + w[1:2] * sh[q:2 * q]
               + w[0:1] * sh[2 * q:3 * q])
        return _silu(acc)

    cw = cw_ref[...]
    cb = cb_ref[...]
    xs = conv(extx, cw[:, :D_INNER], cb[:, :D_INNER])
    bcv = conv(extbc, cw[:, D_INNER:], cb[:, D_INNER:])

    dt = _softplus(dt_ref[...] + dtb_ref[...])
    a = dt * (-jnp.exp(alog_ref[...]))
    row = lax.broadcasted_iota(jnp.int32, (q, q), 0)
    col = lax.broadcasted_iota(jnp.int32, (q, q), 1)
    causal = row >= col
    acs = _sel_dot(jnp.where(causal, 1.0, 0.0).astype(BF16), a, terms=3)
    acs_t = acs.T
    a_last = acs[q - 1:q, :]
    dtx = _expand(dt, e_ref)
    dwx = _expand(dt * jnp.exp(a_last - acs), e_ref)
    eax = _expand(jnp.exp(acs), e_ref)
    xdt = xs * dtx
    xw_b = (xs * dwx).astype(BF16)
    lane = lax.broadcasted_iota(jnp.int32, (q, LANES), 1)
    heads_per_group = SSD_HEADS // SSD_GROUPS

    for g in range(SSD_GROUPS):
        gs = slice(g * GROUP_W, (g + 1) * GROUP_W)
        bg = bcv[:, g * D_STATE:(g + 1) * D_STATE].astype(BF16)
        cg = bcv[:, (SSD_GROUPS + g) * D_STATE:(SSD_GROUPS + g + 1) * D_STATE].astype(BF16)
        cbm = _dot_nt(cg, bg)
        htg = ht[:, gs]
        y_off = _dot(cg, htg.astype(BF16)) * eax[:, gs]
        ht[:, gs] = eax[q - 1:q, gs] * htg + _dot_tn(bg, xw_b[:, gs])
        pairs = []
        for j in range(heads_per_group // 2):
            h0 = g * heads_per_group + 2 * j
            xpair = xdt[:, h0 * SSD_HEAD_DIM:(h0 + 2) * SSD_HEAD_DIM]
            yp = None
            for k in range(2):
                h = h0 + k
                seg = acs[:, h:h + 1] - acs_t[h:h + 1, :]
                m = (cbm * jnp.exp(jnp.where(causal, seg, NEG))).astype(BF16)
                keep = (lane < SSD_HEAD_DIM) if k == 0 else (lane >= SSD_HEAD_DIM)
                t = _dot(m, jnp.where(keep, xpair, 0.0).astype(BF16))
                yp = t if yp is None else yp + t
            pairs.append(yp)
        ygrp = jnp.concatenate(pairs, axis=1) + y_off + d_ref[:, gs] * xs[:, gs]
        _gate_norm_store(y_ref, ygrp, z_ref, gn_ref, g)

    @pl.when(c == pl.num_programs(1) - 1)
    def _():
        hout_ref[0] = ht[...].T


def _ssd_prompt(proj, dt, sp, batch, t):
    nc = t // SSD_CHUNK
    q = SSD_CHUNK
    const = lambda b, c: (0, 0)
    return pl.pallas_call(
        _ssd_prompt_kernel,
        out_shape=(jax.ShapeDtypeStruct((batch * t, D_INNER), BF16),
                   jax.ShapeDtypeStruct((batch, D_INNER, D_STATE), F32)),
        grid=(batch, nc),
        in_specs=[pl.BlockSpec((q, D_INNER), lambda b, c: (b * nc + c, 0)),
                  pl.BlockSpec((q, 1024), lambda b, c: (b * nc + c, 2)),
                  pl.BlockSpec((q, D_INNER), lambda b, c: (b * nc + c, 2)),
                  pl.BlockSpec((q, LANES), lambda b, c: (b * nc + c, 0)),
                  pl.BlockSpec((4, CONV_DIM), const), pl.BlockSpec((1, CONV_DIM), const),
                  pl.BlockSpec((1, LANES), const), pl.BlockSpec((1, LANES), const),
                  pl.BlockSpec((1, D_INNER), const), pl.BlockSpec((1, D_INNER), const),
                  pl.BlockSpec((LANES, D_INNER), const),
                  pl.BlockSpec((3 * q, 2 * q), const)],
        out_specs=(pl.BlockSpec((q, D_INNER), lambda b, c: (b * nc + c, 0)),
                   pl.BlockSpec((1, D_INNER, D_STATE), lambda b, c: (b, 0, 0))),
        scratch_shapes=[pltpu.VMEM((2 * q, D_INNER), BF16),
                        pltpu.VMEM((2 * q, CONV_DIM - D_INNER), BF16),
                        pltpu.VMEM((D_STATE, D_INNER), F32)],
        compiler_params=_cparams(("parallel", "arbitrary"), vmem_mb=48),
        name="ssd_prompt",
    )(proj, proj, proj, dt, sp["conv_w"], sp["conv_b"], sp["dt_bias"], sp["a_log"], sp["d_x"], sp["g_norm"], sp["expand"],
      _conv_shift_matrix(q))


def _conv_shift_matrix(q):
    row = jnp.arange(3 * q, dtype=jnp.int32)[:, None]
    col = jnp.arange(2 * q, dtype=jnp.int32)[None, :]
    return (col == q + row % q - (row // q + 1)).astype(BF16)


def _ssd_sample_kernel(xs_ref, bc_ref, z_ref, dt_ref, cs_ref, h0_ref, cw_ref, cb_ref, dtb_ref, alog_ref, d_ref,
                       gn_ref, e_ref, *rest, n_carried, out_layer):
    y_ref, cnew_ref, hnew_ref, ext = rest[n_carried:]
    t = xs_ref.shape[0]
    for other in range(cnew_ref.shape[0]):
        if other != out_layer:
            cnew_ref[other] = jnp.zeros(cnew_ref.shape[1:], F32)
            hnew_ref[other] = jnp.zeros(hnew_ref.shape[1:], F32)
    ext[5:8] = cs_ref[0]
    ext[8:8 + t, 0:D_INNER] = xs_ref[...]
    ext[8:8 + t, D_INNER:CONV_DIM] = bc_ref[...]
    cw = cw_ref[...]
    acc = (cb_ref[...] + cw[0:1] * ext[5:5 + t] + cw[1:2] * ext[6:6 + t]
           + cw[2:3] * ext[7:7 + t] + cw[3:4] * ext[8:8 + t])
    cnew_ref[out_layer, 0] = ext[5 + t:8 + t]
    xbc = _silu(acc)
    xs = xbc[:, :D_INNER]

    dt = _softplus(dt_ref[...] + dtb_ref[...])
    a = dt * (-jnp.exp(alog_ref[...]))
    rowi = lax.broadcasted_iota(jnp.int32, (t, LANES), 0)
    acs = jnp.zeros((t, LANES), F32)
    for s in range(t):
        acs = acs + jnp.where(rowi >= s, a[s:s + 1, :], 0.0)
    a_last = acs[t - 1:t, :]
    eacs = jnp.exp(acs)
    acsx = _expand(acs, e_ref, terms=3)
    dtx = _expand(dt, e_ref)
    dwx = _expand(dt * jnp.exp(a_last - acs), e_ref)
    eax = _expand(eacs, e_ref)
    xdt = xs * dtx
    xw_b = (xs * dwx).astype(BF16)

    def grp(base, g):
        return xbc[:, D_INNER + (base + g) * D_STATE:D_INNER + (base + g + 1) * D_STATE]

    li = lax.broadcasted_iota(jnp.int32, (t, D_INNER), 0)
    y = jnp.zeros((t, D_INNER), F32)
    for s in range(t):
        decay = jnp.exp(jnp.where(li >= s, acsx - acsx[s:s + 1, :], NEG))
        cbx = jnp.concatenate(
            [jnp.broadcast_to(jnp.sum(grp(SSD_GROUPS, g) * grp(0, g)[s:s + 1, :], axis=-1, keepdims=True),
                              (t, GROUP_W)) for g in range(SSD_GROUPS)], axis=1)
        y = y + cbx * decay * xdt[s:s + 1, :]

    last_only = jnp.where(rowi == t - 1, eacs, 0.0)
    ones = jnp.ones((t, LANES), BF16)
    rdec = sum(_dot_tn(p, ones) for p in _split_bf16(last_only, 3))

    heads_per_group = SSD_HEADS // SSD_GROUPS
    for g in range(SSD_GROUPS):
        gs = slice(g * GROUP_W, (g + 1) * GROUP_W)
        bg = grp(0, g).astype(BF16)
        cg = grp(SSD_GROUPS, g).astype(BF16)
        h0g = h0_ref[0, gs, :]
        y_off = _dot_nt(cg, h0g.astype(BF16)) * eax[:, gs]
        upd = _dot_tn(xw_b[:, gs], bg)
        for hh in range(heads_per_group):
            h = g * heads_per_group + hh
            rs = slice(hh * SSD_HEAD_DIM, (hh + 1) * SSD_HEAD_DIM)
            hnew_ref[out_layer, 0, h * SSD_HEAD_DIM:(h + 1) * SSD_HEAD_DIM, :] = (
                rdec[h:h + 1, :] * h0g[rs, :] + upd[rs, :])
        ygrp = y[:, gs] + y_off + d_ref[:, gs] * xs[:, gs]
        _gate_norm_store(y_ref, ygrp, z_ref, gn_ref, g)


def _ssd_sample(proj, dt, conv_all, ssm_all, stacked, layer, sp, batch, t):
    const = lambda b: (0, 0)
    n_layers = ssm_all.shape[0]
    conv_spec = pl.BlockSpec((None, 1, 3, CONV_DIM), lambda b: (layer, b, 0, 0))
    ssm_spec = pl.BlockSpec((None, 1, D_INNER, D_STATE), lambda b: (layer, b, 0, 0))
    in_specs = [pl.BlockSpec((t, D_INNER), lambda b: (b, 0)),
                pl.BlockSpec((t, 1024), lambda b: (b, 2)),
                pl.BlockSpec((t, D_INNER), lambda b: (b, 2)),
                pl.BlockSpec((t, LANES), lambda b: (b, 0)),
                conv_spec, ssm_spec,
                pl.BlockSpec((4, CONV_DIM), const), pl.BlockSpec((1, CONV_DIM), const),
                pl.BlockSpec((1, LANES), const), pl.BlockSpec((1, LANES), const),
                pl.BlockSpec((1, D_INNER), const), pl.BlockSpec((1, D_INNER), const),
                pl.BlockSpec((LANES, D_INNER), const)]
    args = [proj, proj, proj, dt, conv_all, ssm_all, sp["conv_w"], sp["conv_b"], sp["dt_bias"], sp["a_log"],
            sp["d_x"], sp["g_norm"], sp["expand"]]
    if stacked is None:
        aliases = {}
        out_layer = layer
        conv_out = pl.BlockSpec((n_layers, 1, 3, CONV_DIM), lambda b: (0, b, 0, 0))
        ssm_out = pl.BlockSpec((n_layers, 1, D_INNER, D_STATE), lambda b: (0, b, 0, 0))
    else:
        aliases = {len(args): 1, len(args) + 1: 2}
        in_specs += [pl.BlockSpec(memory_space=pl.ANY), pl.BlockSpec(memory_space=pl.ANY)]
        args += list(stacked)
        out_layer = 0
        conv_out = pl.BlockSpec((1, 1, 3, CONV_DIM), lambda b: (layer, b, 0, 0))
        ssm_out = pl.BlockSpec((1, 1, D_INNER, D_STATE), lambda b: (layer, b, 0, 0))
    y, conv_new, ssm_new = pl.pallas_call(
        functools.partial(_ssd_sample_kernel, n_carried=len(aliases), out_layer=out_layer),
        out_shape=(jax.ShapeDtypeStruct((batch * t, D_INNER), F32),
                   jax.ShapeDtypeStruct((n_layers, batch, 3, CONV_DIM), F32),
                   jax.ShapeDtypeStruct((n_layers, batch, D_INNER, D_STATE), F32)),
        grid=(batch,),
        in_specs=in_specs,
        out_specs=(pl.BlockSpec((t, D_INNER), lambda b: (b, 0)), conv_out, ssm_out),
        scratch_shapes=[pltpu.VMEM((8 + t, CONV_DIM), F32)],
        input_output_aliases=aliases,
        compiler_params=_cparams(("parallel",)),
        name="ssd_sample",
    )(*args)
    return y, (conv_new, ssm_new)


def _flash_kernel(qt_ref, kt_ref, q_ref, k_ref, v_ref, o_ref, m_ref, acc_ref, *, sub):
    step = pl.program_id(2)
    qi = qt_ref[step]
    ki = kt_ref[step]
    n_heads, tq, _ = q_ref.shape
    nsub = tq // sub

    @pl.when(ki == 0)
    def _():
        m_ref[...] = jnp.full(m_ref.shape, NEG, F32)
        acc_ref[...] = jnp.zeros(acc_ref.shape, F32)

    def unit(pair, qs, ks, masked, k_width=1):
        rows = slice(qs * sub, (qs + 1) * sub)
        cols = slice(ks * sub, (ks + k_width) * sub)
        for h in range(2):
            hd = pair * 2 + h
            s = _dot(q_ref[hd, rows, :], k_ref[pl.ds(pl.multiple_of(hd * LANES, LANES), LANES), cols])
            if masked:
                r = lax.broadcasted_iota(jnp.int32, (sub, sub), 0)
                c = lax.broadcasted_iota(jnp.int32, (sub, sub), 1)
                s = jnp.where(r >= c, s, NEG)
            m_prev = m_ref[hd, rows, :]
            m_new = jnp.maximum(m_prev, jnp.max(s, axis=-1, keepdims=True))
            p = jnp.exp2(s - jnp.concatenate([m_new] * (s.shape[1] // LANES), axis=1))
            acc_ref[hd, rows, :] = (acc_ref[hd, rows, :] * jnp.exp2(m_prev - m_new)
                                    + _dot(p.astype(BF16), v_ref[hd, cols, :]))
            m_ref[hd, rows, :] = m_new

    def head_pairs(diagonal):
        def body(pair, carry):
            if diagonal:
                for ks in range(nsub):
                    for qs in range(ks, nsub):
                        unit(pair, qs, ks, ks == qs)
            else:
                for qs in range(nsub):
                    unit(pair, qs, 0, False, k_width=nsub)
            return carry
        lax.fori_loop(0, n_heads // 2, body, 0)

    @pl.when(ki < qi)
    def _():
        head_pairs(False)

    @pl.when(ki == qi)
    def _():
        head_pairs(True)
        for pair in range(n_heads // 2):
            outs = []
            for hd in (2 * pair, 2 * pair + 1):
                a = acc_ref[hd]
                outs.append((a * (1.0 / a[:, V_HEAD:V_HEAD + 1]))[:, :V_HEAD])
            o_ref[:, pair * LANES:(pair + 1) * LANES] = jnp.concatenate(outs, axis=1).astype(o_ref.dtype)


def _flash_attend(q2, k_all, v_all, batch, t):
    tq = min(ATT_TILE, t)
    nq = t // tq
    pairs = [(qi, ki) for qi in range(nq) for ki in range(qi + 1)]
    qt = jnp.asarray([p[0] for p in pairs], jnp.int32)
    kt = jnp.asarray([p[1] for p in pairs], jnp.int32)
    hps = FLASH_HEADS_PER_STEP
    grid_spec = pltpu.PrefetchScalarGridSpec(
        num_scalar_prefetch=2,
        grid=(batch, MLA_HEADS // hps, len(pairs)),
        in_specs=[pl.BlockSpec((hps, tq, LANES), lambda b, hg, s, qt, kt: (hg, b * nq + qt[s], 0)),
                  pl.BlockSpec((hps * LANES, tq), lambda b, hg, s, qt, kt: (hg, b * nq + kt[s])),
                  pl.BlockSpec((hps, tq, LANES), lambda b, hg, s, qt, kt: (hg, b * nq + kt[s], 0))],
        out_specs=pl.BlockSpec((tq, hps * V_HEAD), lambda b, hg, s, qt, kt: (b * nq + qt[s], hg)),
        scratch_shapes=[pltpu.VMEM((hps, tq, LANES), F32), pltpu.VMEM((hps, tq, LANES), F32)],
    )
    return pl.pallas_call(
        functools.partial(_flash_kernel, sub=min(ATT_SUB, tq)),
        out_shape=jax.ShapeDtypeStruct((batch * t, MLA_HEADS * V_HEAD), BF16),
        grid_spec=grid_spec,
        compiler_params=_cparams(("parallel", "parallel", "arbitrary")),
        name="mla_flash",
    )(qt, kt, q2, k_all, v_all)


def _decode_kernel(pt_ref, q_ref, cn_ref, pn_ref, ckv_hbm, kpe_hbm, o_ref, ckv_buf, kpe_buf, sem, kv_ref, s_ref,
                   *, n_pages, page, chunk):
    b = pl.program_id(0)
    last = pl.num_programs(0) - 1
    slot = b % 2
    rows = q_ref.shape[1]
    n_chunks = n_pages * page // chunk
    pages_per_chunk = n_pages // n_chunks

    def page_copies(pid, sl, p):
        rows_p = pl.ds(p * page, page)
        return (pltpu.make_async_copy(ckv_hbm.at[pid], ckv_buf.at[sl, rows_p, :], sem.at[0, sl]),
                pltpu.make_async_copy(kpe_hbm.at[pid], kpe_buf.at[sl, :, rows_p], sem.at[1, sl]))

    def start_pages(bb, sl, p0, p1):
        for p in range(p0, p1):
            for cp in page_copies(pt_ref[bb * n_pages + p], sl, p):
                cp.start()

    def wait_pages(sl):
        for p in range(n_pages):
            for cp in page_copies(0, sl, p):
                cp.wait()

    @pl.when(b == 0)
    def _():
        start_pages(0, 0, 0, n_pages)

    wait_pages(slot)
    nxt = jnp.minimum(b + 1, last)

    q = q_ref[0]
    ql = q[:, :KV_LORA]
    qp = q[:, KV_LORA:KV_LORA + QK_ROPE]
    past = n_pages * page

    for c in range(n_chunks):
        start_pages(nxt, 1 - slot, c * pages_per_chunk, (c + 1) * pages_per_chunk)
        cs = slice(c * chunk, (c + 1) * chunk)
        kc = ckv_buf[slot, cs, :].astype(BF16)
        kp = kpe_buf[slot, :, cs].astype(BF16)
        kv_ref[cs, :] = kc
        s_ref[:, cs] = _dot_nt(ql, kc) + _dot(qp, kp)

    t_new = cn_ref.shape[1]
    pad = jnp.zeros((LANES - t_new, KV_LORA), F32)
    kc = jnp.concatenate([cn_ref[0], pad], axis=0).astype(BF16)
    kp = jnp.concatenate([pn_ref[0], pad[:, :LANES]], axis=0).astype(BF16)
    s = _dot_nt(ql, kc) + _dot_nt(q[:, KV_LORA:], kp)
    r = lax.broadcasted_iota(jnp.int32, (rows, LANES), 0)
    c = lax.broadcasted_iota(jnp.int32, (rows, LANES), 1)
    kv_ref[past:, :] = kc
    s_ref[:, past:] = jnp.where(c <= r // MLA_HEADS, s, NEG)

    s_all = s_ref[...]
    p = jnp.exp2(s_all - jnp.max(s_all, axis=-1, keepdims=True))
    l = jnp.sum(p, axis=-1, keepdims=True)
    o_ref[0] = (_dot(p.astype(BF16), kv_ref[...]) / l).astype(o_ref.dtype)

    @pl.when(b == last)
    def _():
        wait_pages(1 - slot)


def _decode_attend(q_abs, ckv_new, kpe_new, cache_ckv, cache_kpe_t, page_table):
    batch, rows, _ = q_abs.shape
    n_pages = page_table.shape[1]
    page = cache_ckv.shape[1]
    past = n_pages * page
    t_new = ckv_new.shape[1]
    per_b = lambda b, pt: (b, 0, 0)
    grid_spec = pltpu.PrefetchScalarGridSpec(
        num_scalar_prefetch=1,
        grid=(batch,),
        in_specs=[pl.BlockSpec((1, rows, LAT_W), per_b),
                  pl.BlockSpec((1, t_new, KV_LORA), per_b),
                  pl.BlockSpec((1, t_new, LANES), per_b),
                  pl.BlockSpec(memory_space=pl.ANY),
                  pl.BlockSpec(memory_space=pl.ANY)],
        out_specs=pl.BlockSpec((1, rows, KV_LORA), per_b),
        scratch_shapes=[pltpu.VMEM((2, past, KV_LORA), F32),
                        pltpu.VMEM((2, QK_ROPE, past), F32),
                        pltpu.SemaphoreType.DMA((2, 2)),
                        pltpu.VMEM((past + LANES, KV_LORA), BF16),
                        pltpu.VMEM((rows, past + LANES), F32)],
    )
    return pl.pallas_call(
        functools.partial(_decode_kernel, n_pages=n_pages, page=page, chunk=min(DECODE_CHUNK, past)),
        out_shape=jax.ShapeDtypeStruct((batch, rows, KV_LORA), BF16),
        grid_spec=grid_spec,
        compiler_params=_cparams(("arbitrary",)),
        name="mla_decode",
    )(page_table.reshape(-1), q_abs, ckv_new, kpe_new, cache_ckv, cache_kpe_t)


def _pad_lanes(v, width):
    return jnp.pad(v, [(0, 0)] * (v.ndim - 1) + [(0, width - v.shape[-1])])


def _ssd_params(i, w_in_a, conv_w, conv_b, dt_bias, a_log, d_skip, g_ssd_norm):
    w = w_in_a[i]
    z0, x0, d0, m0 = 0, D_INNER, D_INNER + CONV_DIM, D_INNER + CONV_DIM + SSD_HEADS
    w_main = jnp.concatenate([w[:, x0:d0], w[:, m0:], w[:, z0:x0]], axis=1).astype(BF16)
    w_dt = _pad_lanes(w[:, d0:m0], LANES).astype(BF16)
    head_of = jnp.arange(D_INNER, dtype=jnp.int32) // SSD_HEAD_DIM
    expand = (jnp.arange(LANES, dtype=jnp.int32)[:, None] == head_of[None, :]).astype(BF16)
    return dict(w_main=w_main, w_dt=w_dt, conv_w=conv_w[i], conv_b=conv_b[i].reshape(1, CONV_DIM),
                dt_bias=_pad_lanes(dt_bias[i].reshape(1, SSD_HEADS), LANES),
                a_log=_pad_lanes(a_log[i].reshape(1, SSD_HEADS), LANES),
                d_x=jnp.repeat(d_skip[i], SSD_HEAD_DIM).reshape(1, D_INNER),
                g_norm=g_ssd_norm[i].reshape(1, D_INNER), expand=expand)


def _mla_shared_params(w_kv_a, w_uk, w_uv):
    half = QK_ROPE // 2
    wa = _pad_lanes(w_kv_a, LAT_W).astype(BF16)
    pe = w_kv_a[:, KV_LORA:]
    wr = _pad_lanes(jnp.concatenate([-pe[:, half:], pe[:, :half]], axis=1), LANES).astype(BF16)
    eye = jnp.eye(QK_ROPE, dtype=F32)
    k_blocks, abs_blocks = [], []
    for h in range(MLA_HEADS):
        kb = jnp.zeros((LAT_W, LANES), F32)
        kb = kb.at[:KV_LORA, :QK_NOPE].set(w_uk[:, h, :])
        kb = kb.at[KV_LORA:KV_LORA + QK_ROPE, QK_NOPE:QK_NOPE + QK_ROPE].set(eye)
        k_blocks.append(kb)
        ab = jnp.zeros((LANES, LAT_W), F32)
        ab = ab.at[:QK_NOPE, :KV_LORA].set(w_uk[:, h, :].T)
        ab = ab.at[QK_NOPE:QK_NOPE + QK_ROPE, KV_LORA:KV_LORA + QK_ROPE].set(eye)
        abs_blocks.append(ab)
    w_k = jnp.concatenate(k_blocks, axis=1).T.astype(BF16)
    v_blocks = []
    for h in range(MLA_HEADS):
        vb = jnp.zeros((LAT_W, LANES), F32).at[:KV_LORA, :V_HEAD].set(w_uv[:, h, :])
        v_blocks.append(vb.at[KV_LORA + QK_ROPE, V_HEAD].set(1.0))
    w_v = jnp.concatenate(v_blocks, axis=1).astype(BF16)
    w_abs = jnp.stack(abs_blocks).astype(BF16)
    zero = jnp.zeros((KV_LORA, V_HEAD), F32)
    w_uvp = jnp.stack([jnp.block([[w_uv[:, 2 * j, :], zero], [zero, w_uv[:, 2 * j + 1, :]]])
                       for j in range(MLA_HEADS // 2)]).astype(BF16)
    return dict(wa=wa, wr=wr, w_k=w_k, w_v=w_v, w_abs=w_abs, w_uvp=w_uvp)


def _mla_layer_params(i, w_in_b, w_q_b):
    half = QK_ROPE // 2
    w = w_in_b[i]
    w_inb = jnp.concatenate([w[:, Q_LORA:], w[:, :Q_LORA]], axis=1).astype(BF16)
    wq = w_q_b[i]
    wqa = _pad_lanes(wq, LANES).reshape(Q_LORA, MLA_HEADS * LANES).astype(BF16)
    rot = jnp.concatenate([jnp.zeros_like(wq[..., :QK_NOPE]), -wq[..., QK_NOPE + half:], wq[..., QK_NOPE:QK_NOPE + half]],
                          axis=-1)
    wqb = _pad_lanes(rot, LANES).reshape(Q_LORA, MLA_HEADS * LANES).astype(BF16)
    return dict(w_inb=w_inb, wqa=wqa, wqb=wqb)


def _rope_tables(pos):
    half = QK_ROPE // 2
    inv = ROPE_BASE ** (-jnp.arange(half, dtype=F32) / half)
    ang = pos.astype(F32)[:, None] * inv[None, :]
    cos, sin = jnp.cos(ang), jnp.sin(ang)
    n = pos.shape[0]
    one = lambda w: jnp.ones((n, w), F32)
    zero = lambda w: jnp.zeros((n, w), F32)
    tail = LANES - QK_NOPE - QK_ROPE
    cos_q = jnp.concatenate([one(QK_NOPE), cos, cos, one(tail)], axis=1)
    sin_q = jnp.concatenate([zero(QK_NOPE), sin, sin, zero(tail)], axis=1)
    cos_k = jnp.concatenate([cos, cos, one(LANES - QK_ROPE)], axis=1)
    sin_k = jnp.concatenate([sin, sin, zero(LANES - QK_ROPE)], axis=1)
    return cos_q, sin_q, cos_k, sin_k


def _trunk(x, batch, t, sample, mem_attend, ssd_state, mla_cache, tabs, P):
    act = F32 if sample else BF16
    cos_q, sin_q, cos_k, sin_k = tabs
    ssm_out, conv_out, stacked = [], [], None
    ckv = kpe = None
    for l in range(DEPTH):
        if l < N_A:
            sp = P["ssd"][l]
            proj = _proj(x, sp["w_main"], g=P["g_pre_mix"][l], out_dtype=act, tn=1024, row_tiles=2, name=f"in_a{l}")
            dt = _proj(x, sp["w_dt"], g=P["g_pre_mix"][l], out_dtype=F32, name=f"in_dt{l}")
            if sample:
                state_ssm, state_conv = ssd_state
                y, stacked = _ssd_sample(proj, dt, state_conv, state_ssm.reshape(N_A, batch, D_INNER, D_STATE),
                                         stacked, l, sp, batch, t)
            else:
                y, h_new = _ssd_prompt(proj, dt, sp, batch, t)
                ssm_out.append(h_new)
                conv_out.append(proj.reshape(batch, t, -1)[:, t - 3:, :CONV_DIM].astype(F32))
            mo = mem_attend(l, proj, 3)
            w_out = P["w_out_a"][l]
            x = _outproj(y, mo, w_out[:D_INNER], w_out[D_INNER:], P["g_post_mix"][l], x, f"out_a{l}")
        else:
            i = l - N_A
            ms, ml = P["mla_shared"], P["mla"][i]
            if i == 0:
                ckv, kpe, lat = _kv_latent(x, P["g_kv_in"], ms["wa"], ms["wr"], P["g_kv_norm"], cos_k, sin_k)
                if not sample:
                    k_all = _proj_t(lat, ms["w_k"], "k_heads")
                    v_all = _proj(lat, ms["w_v"], tn=1024, head_major=True, name="v_heads")
            proj = _proj(x, ml["w_inb"], g=P["g_pre_mix"][l], out_dtype=act, name=f"in_b{i}")
            q2 = _proj(proj, ml["wqa"], g=P["g_q_norm"][i], xcol=2, tn=1024,
                       rope=(ml["wqb"], cos_q, sin_q, MLA_SCALE * LOG2E), head_major=not sample,
                       name=f"q_heads{i}")
            if sample:
                cache_ckv, cache_kpe, page_table = mla_cache
                q_abs = _block_proj(q2, ms["w_abs"], f"q_abs{i}").reshape(batch, t * MLA_HEADS, LAT_W)
                o_lat = _decode_attend(q_abs, ckv.reshape(batch, t, KV_LORA), kpe.reshape(batch, t, LANES),
                                       cache_ckv, cache_kpe, page_table)
                o = _block_proj(o_lat.reshape(batch * t, MLA_HEADS * KV_LORA), ms["w_uvp"], f"o_heads{i}")
            else:
                o = _flash_attend(q2, k_all, v_all, batch, t)
            mo = mem_attend(l, proj, 0)
            w_out = P["w_out_b"][i]
            n_o = MLA_HEADS * V_HEAD
            x = _outproj(o, mo, w_out[:n_o], w_out[n_o:], P["g_post_mix"][l], x, f"out_b{i}")
        x = _ffn(x, P["g_pre_ffn"][l], P["w_ffn_up"][l], P["w_ffn_down"][l], P["g_post_ffn"][l], f"ffn{l}")
    conv_new, ssm_new = stacked if sample else (jnp.stack(conv_out), jnp.stack(ssm_out))
    ssm_new = ssm_new.reshape(N_A, batch, SSD_HEADS, SSD_HEAD_DIM, D_STATE)
    return x, ssm_new, conv_new, ckv, kpe[:, :QK_ROPE]


def kernel(x_prompt, x_sample, mem_prompt, state_ssm, state_conv, cache_ckv, cache_kpe, cache_mem_k, cache_mem_v, page_table, g_pre_mix, g_post_mix, g_pre_ffn, g_post_ffn, w_ffn_up, w_ffn_down, w_mem_k, w_mem_v, w_in_a, conv_w, conv_b, dt_bias, a_log, d_skip, g_ssd_norm, w_out_a, g_kv_in, w_kv_a, g_kv_norm, w_uk, w_uv, w_in_b, g_q_norm, w_q_b, w_out_b):
    b, s, _ = x_prompt.shape
    db, t, _ = x_sample.shape
    past_len = page_table.shape[1] * cache_ckv.shape[1]

    P = dict(
        g_pre_mix=g_pre_mix, g_post_mix=g_post_mix, g_pre_ffn=g_pre_ffn, g_post_ffn=g_post_ffn,
        g_kv_in=g_kv_in, g_kv_norm=g_kv_norm, g_q_norm=g_q_norm,
        w_ffn_up=w_ffn_up.astype(BF16), w_ffn_down=w_ffn_down.astype(BF16),
        w_out_a=w_out_a.astype(BF16), w_out_b=w_out_b.astype(BF16),
        ssd=[_ssd_params(i, w_in_a, conv_w, conv_b, dt_bias, a_log, d_skip, g_ssd_norm) for i in range(N_A)],
        mla_shared=_mla_shared_params(w_kv_a, w_uk, w_uv),
        mla=[_mla_layer_params(i, w_in_b, w_q_b) for i in range(DEPTH - N_A)],
    )

    w_mem = jnp.concatenate([w_mem_k[l] for l in range(DEPTH)] + [w_mem_v[l] for l in range(DEPTH)], axis=1)
    mem_kv = _proj(mem_prompt.reshape(b * N_MEM, D_MODEL), w_mem.astype(BF16), out_dtype=F32, tn=1024, name="mem_kv")
    mem_kv5 = mem_kv.reshape(b, N_MEM, 2 * DEPTH, MEM_HEADS, MEM_HEAD_DIM)
    p_mem_k = jnp.transpose(mem_kv5[:, :, :DEPTH], (2, 0, 1, 3, 4))
    p_mem_v = jnp.transpose(mem_kv5[:, :, DEPTH:], (2, 0, 1, 3, 4))
    mem_kv3 = mem_kv.reshape(b, N_MEM, 2 * DEPTH * MEM_WIDTH)
    tabs_p = _rope_tables(jnp.arange(s, dtype=jnp.int32))
    y_prompt, p_ssm, p_conv, p_ckv, p_kpe = _trunk(
        x_prompt.reshape(b * s, D_MODEL), b, s, False,
        lambda l, q, qcol: _mem_attend(q, qcol, mem_kv3, mem_kv3, 0, l, DEPTH + l, b, s, BF16, f"mem{l}"),
        None, None, tabs_p, P)

    tabs_s = tuple(jnp.tile(tb, (db, 1)) for tb in _rope_tables(past_len + jnp.arange(t, dtype=jnp.int32)))
    cache_kpe_t = jnp.swapaxes(cache_kpe, 1, 2)
    y_sample, s_ssm, s_conv, s_ckv, s_kpe = _trunk(
        x_sample.reshape(db * t, D_MODEL), db, t, True,
        lambda l, q, qcol: _mem_attend_cache(q, qcol, cache_mem_k, cache_mem_v, l, db, t, f"mem_s{l}"),
        (state_ssm, state_conv), (cache_ckv, cache_kpe_t, page_table), tabs_s, P)

    return (y_prompt.reshape(b, s, D_MODEL), y_sample.reshape(db, t, D_MODEL),
            p_ssm, p_conv, p_ckv.reshape(b, s, KV_LORA), p_kpe.reshape(b, s, QK_ROPE), p_mem_k, p_mem_v,
            s_ssm, s_conv, s_ckv.reshape(db, t, KV_LORA), s_kpe.reshape(db, t, QK_ROPE))
```

```python
import functools

import jax
import jax.numpy as jnp
from jax import lax
from jax.experimental import pallas as pl
from jax.experimental.pallas import tpu as pltpu

F32 = jnp.float32
BF16 = jnp.bfloat16

D_MODEL = 1024
DEPTH = 4
N_A = 2
D_INNER = 2048
SSD_HEADS = 32
SSD_HEAD_DIM = 64
SSD_GROUPS = 4
GROUP_W = D_INNER // SSD_GROUPS
D_STATE = 128
CONV_DIM = 3072
SSD_CHUNK = 128
MLA_HEADS = 16
Q_LORA = 512
KV_LORA = 256
QK_NOPE = 64
QK_ROPE = 32
V_HEAD = 64
ROPE_BASE = 10000.0
MLA_SCALE = (QK_NOPE + QK_ROPE) ** -0.5
LOG2E = 1.4426950408889634
N_MEM = 256
MEM_HEADS = 4
MEM_HEAD_DIM = 256
MEM_WIDTH = 1024
D_FF = 4096
RMS_EPS = 1e-6
LANES = 128
LAT_W = 384
NEG = -1e30
ROW_TILE = 512
ATT_TILE = 2048
ATT_SUB = 512
FLASH_HEADS_PER_STEP = 4
DECODE_CHUNK = 1024
MEM_BATCH_BLOCK = 4

NT_DIMS = (((1,), (1,)), ((), ()))
TN_DIMS = (((0,), (0,)), ((), ()))


def _cparams(sem, vmem_mb=48):
    return pltpu.CompilerParams(dimension_semantics=sem, vmem_limit_bytes=vmem_mb * 1024 * 1024)


def _rms(x, g):
    ms = jnp.mean(x * x, axis=-1, keepdims=True)
    return x * lax.rsqrt(ms + RMS_EPS) * g


def _silu(x):
    return x * (1.0 / (1.0 + jnp.exp(-x)))


def _softplus(x):
    return jnp.maximum(x, 0.0) + jnp.log1p(jnp.exp(-jnp.abs(x)))


def _dot(a, b):
    return jnp.dot(a, b, preferred_element_type=F32)


def _dot_nt(a, b):
    return lax.dot_general(a, b, NT_DIMS, preferred_element_type=F32)


def _dot_tn(a, b):
    return lax.dot_general(a, b, TN_DIMS, preferred_element_type=F32)


def _split_bf16(v, terms):
    parts = []
    for _ in range(terms):
        p = v.astype(BF16)
        parts.append(p)
        v = v - p.astype(F32)
    return parts


def _sel_dot(m, v, terms=2):
    return sum(_dot(m, p) for p in _split_bf16(v, terms))


def _expand(v, e_ref, terms=2):
    return sum(_dot(p, e_ref[...]) for p in _split_bf16(v, terms))


def _proj_kernel(*refs, has_g, has_rope, scale):
    it = iter(refs)
    x_ref = next(it)
    g_ref = next(it) if has_g else None
    w_ref = next(it)
    if has_rope:
        w2_ref, cos_ref, sin_ref = next(it), next(it), next(it)
    o_ref = next(it)
    xn_ref = next(it)

    @pl.when(pl.program_id(1) == 0)
    def _():
        x = x_ref[...].astype(F32)
        if has_g:
            x = _rms(x, g_ref[...])
        xn_ref[...] = x.astype(BF16)

    xn = xn_ref[...]
    y = _dot(xn, w_ref[...])
    if has_rope:
        rep = y.shape[1] // LANES
        cos = jnp.concatenate([cos_ref[...]] * rep, axis=1)
        sin = jnp.concatenate([sin_ref[...]] * rep, axis=1)
        y = (y * cos + _dot(xn, w2_ref[...]) * sin) * scale
    if len(o_ref.shape) == 3:
        for hh in range(o_ref.shape[0]):
            o_ref[hh] = y[:, hh * LANES:(hh + 1) * LANES].astype(o_ref.dtype)
    else:
        o_ref[...] = y.astype(o_ref.dtype)


def _proj(x, w, *, g=None, xcol=0, out_dtype=BF16, tn=None, rope=None, row_tiles=1, head_major=False, name):
    n = x.shape[0]
    k, nout = w.shape
    tm = min(ROW_TILE * row_tiles, n)
    tn = nout if tn is None else tn
    if head_major:
        out_shape = jax.ShapeDtypeStruct((nout // LANES, n, LANES), out_dtype)
        out_spec = pl.BlockSpec((tn // LANES, tm, LANES), lambda i, j: (j, i, 0))
    else:
        out_shape = jax.ShapeDtypeStruct((n, nout), out_dtype)
        out_spec = pl.BlockSpec((tm, tn), lambda i, j: (i, j))
    in_specs = [pl.BlockSpec((tm, k), lambda i, j: (i, xcol))]
    args = [x]
    if g is not None:
        in_specs.append(pl.BlockSpec((1, k), lambda i, j: (0, 0)))
        args.append(g.reshape(1, k).astype(F32))
    in_specs.append(pl.BlockSpec((k, tn), lambda i, j: (0, j)))
    args.append(w)
    scale = 1.0
    if rope is not None:
        w2, cos, sin, scale = rope
        nb = cos.shape[0] // tm
        in_specs.append(pl.BlockSpec((k, tn), lambda i, j: (0, j)))
        in_specs.append(pl.BlockSpec((tm, LANES), lambda i, j: (i % nb, 0)))
        in_specs.append(pl.BlockSpec((tm, LANES), lambda i, j: (i % nb, 0)))
        args += [w2, cos, sin]
    return pl.pallas_call(
        functools.partial(_proj_kernel, has_g=g is not None, has_rope=rope is not None, scale=scale),
        out_shape=out_shape,
        grid=(n // tm, nout // tn),
        in_specs=in_specs,
        out_specs=out_spec,
        scratch_shapes=[pltpu.VMEM((tm, k), BF16)],
        compiler_params=_cparams(("parallel", "arbitrary")),
        name=name,
    )(*args)


def _proj_t_kernel(x_ref, w_ref, o_ref):
    o_ref[...] = _dot_nt(w_ref[...], x_ref[...]).astype(o_ref.dtype)


def _proj_t(x, w_t, name):
    n, k = x.shape
    nout = w_t.shape[0]
    tm = min(ROW_TILE, n)
    return pl.pallas_call(
        _proj_t_kernel,
        out_shape=jax.ShapeDtypeStruct((nout, n), BF16),
        grid=(n // tm,),
        in_specs=[pl.BlockSpec((tm, k), lambda i: (i, 0)), pl.BlockSpec((nout, k), lambda i: (0, 0))],
        out_specs=pl.BlockSpec((nout, tm), lambda i: (0, i)),
        compiler_params=_cparams(("parallel",)),
        name=name,
    )(x, w_t)


def _block_proj_kernel(x_ref, w_ref, o_ref):
    o_ref[...] = _dot(x_ref[...].astype(BF16), w_ref[0]).astype(o_ref.dtype)


def _block_proj(x, w, name):
    n = x.shape[0]
    nb, kx, ko = w.shape
    tm = min(ROW_TILE, n)
    return pl.pallas_call(
        _block_proj_kernel,
        out_shape=jax.ShapeDtypeStruct((n, nb * ko), BF16),
        grid=(n // tm, nb),
        in_specs=[pl.BlockSpec((tm, kx), lambda i, j: (i, j)), pl.BlockSpec((1, kx, ko), lambda i, j: (j, 0, 0))],
        out_specs=pl.BlockSpec((tm, ko), lambda i, j: (i, j)),
        compiler_params=_cparams(("parallel", "parallel")),
        name=name,
    )(x, w)


def _kvlat_kernel(x_ref, g_ref, wa_ref, wr_ref, gkv_ref, cos_ref, sin_ref, ckv_ref, kpe_ref, lat_ref):
    xn = _rms(x_ref[...], g_ref[...]).astype(BF16)
    a = _dot(xn, wa_ref[...])
    r = _dot(xn, wr_ref[...])
    ckv = _rms(a[:, :KV_LORA], gkv_ref[...])
    pe = a[:, KV_LORA:] * cos_ref[...] + r * sin_ref[...]
    ckv_ref[...] = ckv
    kpe_ref[...] = pe
    lat_ref[:, :KV_LORA] = ckv.astype(BF16)
    lane = lax.broadcasted_iota(jnp.int32, pe.shape, 1)
    lat_ref[:, KV_LORA:] = jnp.where(lane == QK_ROPE, 1.0, pe).astype(BF16)


def _kv_latent(x, g, wa, wr, gkv, cos, sin):
    n = x.shape[0]
    tm = min(ROW_TILE, n)
    nb = cos.shape[0] // tm
    row = lambda i: (i, 0)
    const = lambda i: (0, 0)
    return pl.pallas_call(
        _kvlat_kernel,
        out_shape=(jax.ShapeDtypeStruct((n, KV_LORA), F32),
                   jax.ShapeDtypeStruct((n, LANES), F32),
                   jax.ShapeDtypeStruct((n, LAT_W), BF16)),
        grid=(n // tm,),
        in_specs=[pl.BlockSpec((tm, D_MODEL), row),
                  pl.BlockSpec((1, D_MODEL), const),
                  pl.BlockSpec((D_MODEL, LAT_W), const),
                  pl.BlockSpec((D_MODEL, LANES), const),
                  pl.BlockSpec((1, KV_LORA), const),
                  pl.BlockSpec((tm, LANES), lambda i: (i % nb, 0)),
                  pl.BlockSpec((tm, LANES), lambda i: (i % nb, 0))],
        out_specs=(pl.BlockSpec((tm, KV_LORA), row),
                   pl.BlockSpec((tm, LANES), row),
                   pl.BlockSpec((tm, LAT_W), row)),
        compiler_params=_cparams(("parallel",)),
        name="kv_latent",
    )(x, g.reshape(1, D_MODEL), wa, wr, gkv.reshape(1, KV_LORA), cos, sin)


def _outproj_kernel(a1_ref, a2_ref, w1_ref, w2_ref, g_ref, r_ref, o_ref):
    acc = _dot(a1_ref[...].astype(BF16), w1_ref[...]) + _dot(a2_ref[...].astype(BF16), w2_ref[...])
    o_ref[...] = r_ref[...] + _rms(acc, g_ref[...])


def _outproj(a1, a2, w1, w2, g, res, name):
    n = res.shape[0]
    tm = min(ROW_TILE, n)
    k1, k2 = w1.shape[0], w2.shape[0]
    row = lambda i: (i, 0)
    const = lambda i: (0, 0)
    return pl.pallas_call(
        _outproj_kernel,
        out_shape=jax.ShapeDtypeStruct((n, D_MODEL), F32),
        grid=(n // tm,),
        in_specs=[pl.BlockSpec((tm, k1), row), pl.BlockSpec((tm, k2), row),
                  pl.BlockSpec((k1, D_MODEL), const), pl.BlockSpec((k2, D_MODEL), const),
                  pl.BlockSpec((1, D_MODEL), const), pl.BlockSpec((tm, D_MODEL), row)],
        out_specs=pl.BlockSpec((tm, D_MODEL), row),
        compiler_params=_cparams(("parallel",)),
        name=name,
    )(a1, a2, w1, w2, g.reshape(1, D_MODEL), res)


def _ffn_kernel(x_ref, g1_ref, wu_ref, wd_ref, g2_ref, o_ref):
    x = x_ref[...]
    xn = _rms(x, g1_ref[...]).astype(BF16)
    acc = jnp.zeros(x.shape, F32)
    for c in range(D_FF // D_MODEL):
        sl = slice(c * D_MODEL, (c + 1) * D_MODEL)
        h = _dot(xn, wu_ref[:, sl])
        h = jnp.square(jnp.maximum(h, 0.0)).astype(BF16)
        acc = acc + _dot(h, wd_ref[sl, :])
    o_ref[...] = x + _rms(acc, g2_ref[...])


def _ffn(x, g1, wu, wd, g2, name):
    n = x.shape[0]
    tm = min(ROW_TILE, n)
    row = lambda i: (i, 0)
    const = lambda i: (0, 0)
    return pl.pallas_call(
        _ffn_kernel,
        out_shape=jax.ShapeDtypeStruct((n, D_MODEL), F32),
        grid=(n // tm,),
        in_specs=[pl.BlockSpec((tm, D_MODEL), row), pl.BlockSpec((1, D_MODEL), const),
                  pl.BlockSpec((D_MODEL, D_FF), const, pipeline_mode=pl.Buffered(1)),
                  pl.BlockSpec((D_FF, D_MODEL), const, pipeline_mode=pl.Buffered(1)),
                  pl.BlockSpec((1, D_MODEL), const)],
        out_specs=pl.BlockSpec((tm, D_MODEL), row),
        compiler_params=_cparams(("parallel",), vmem_mb=48),
        name=name,
    )(x, g1.reshape(1, D_MODEL), wu, wd, g2.reshape(1, D_MODEL))


def _mem_head(qh, kh, vh):
    s = _dot_nt(qh.astype(BF16), kh.astype(BF16)) * (MEM_HEAD_DIM ** -0.5)
    m = jnp.max(s, axis=-1, keepdims=True)
    p = jnp.exp(s - m)
    l = jnp.sum(p, axis=-1, keepdims=True)
    return _dot(p.astype(BF16), vh.astype(BF16)) / l


def _mem_kernel(q_ref, k_ref, v_ref, o_ref):
    q = q_ref[...]
    for h in range(MEM_HEADS):
        sl = slice(h * MEM_HEAD_DIM, (h + 1) * MEM_HEAD_DIM)
        o_ref[:, sl] = _mem_head(q[:, sl], k_ref[0, :, sl], v_ref[0, :, sl]).astype(o_ref.dtype)


def _mem_cache_kernel(q_ref, k_hbm, v_hbm, o_ref, kbuf, vbuf, sem, *, layer, bb, t):
    step = pl.program_id(0)
    slot = step % 2

    def copies(st, sl):
        out = []
        for i in range(bb):
            for h in range(MEM_HEADS):
                out.append(pltpu.make_async_copy(k_hbm.at[layer, st * bb + i, :, h, :], kbuf.at[sl, i, h], sem.at[0, sl]))
                out.append(pltpu.make_async_copy(v_hbm.at[layer, st * bb + i, :, h, :], vbuf.at[sl, i, h], sem.at[1, sl]))
        return out

    @pl.when(step == 0)
    def _():
        for cp in copies(0, 0):
            cp.start()

    @pl.when(step + 1 < pl.num_programs(0))
    def _():
        for cp in copies(step + 1, 1 - slot):
            cp.start()

    for cp in copies(step, slot):
        cp.wait()

    r = lax.broadcasted_iota(jnp.int32, (MEM_HEADS * t, MEM_HEADS * N_MEM), 0)
    c = lax.broadcasted_iota(jnp.int32, (MEM_HEADS * t, MEM_HEADS * N_MEM), 1)
    same_head = (r // t) == (c // N_MEM)
    for i in range(bb):
        rows = slice(i * t, (i + 1) * t)
        qs = jnp.concatenate([q_ref[rows, h * MEM_HEAD_DIM:(h + 1) * MEM_HEAD_DIM] for h in range(MEM_HEADS)], axis=0)
        ks = jnp.concatenate([kbuf[slot, i, h].astype(BF16) for h in range(MEM_HEADS)], axis=0)
        vs = jnp.concatenate([vbuf[slot, i, h].astype(BF16) for h in range(MEM_HEADS)], axis=0)
        s = jnp.where(same_head, _dot_nt(qs.astype(BF16), ks) * (MEM_HEAD_DIM ** -0.5), NEG)
        p = jnp.exp(s - jnp.max(s, axis=-1, keepdims=True))
        o = _dot(p.astype(BF16), vs) / jnp.sum(p, axis=-1, keepdims=True)
        for h in range(MEM_HEADS):
            o_ref[rows, h * MEM_HEAD_DIM:(h + 1) * MEM_HEAD_DIM] = o[h * t:(h + 1) * t, :].astype(o_ref.dtype)


def _mem_attend_cache(q, qcol, cache_k, cache_v, layer, batch, t, name):
    bb = min(MEM_BATCH_BLOCK, batch)
    buf = pltpu.VMEM((2, bb, MEM_HEADS, N_MEM, MEM_HEAD_DIM), F32)
    return pl.pallas_call(
        functools.partial(_mem_cache_kernel, layer=layer, bb=bb, t=t),
        out_shape=jax.ShapeDtypeStruct((batch * t, MEM_WIDTH), F32),
        grid=(batch // bb,),
        in_specs=[pl.BlockSpec((bb * t, MEM_WIDTH), lambda i: (i, qcol)),
                  pl.BlockSpec(memory_space=pl.ANY), pl.BlockSpec(memory_space=pl.ANY)],
        out_specs=pl.BlockSpec((bb * t, MEM_WIDTH), lambda i: (i, 0)),
        scratch_shapes=[buf, buf, pltpu.SemaphoreType.DMA((2, 2))],
        compiler_params=_cparams(("arbitrary",)),
        name=name,
    )(q, cache_k, cache_v)


def _mem_attend(q, qcol, k_arr, v_arr, kboff, kcol, vcol, batch, t, out_dtype, name):
    tq = min(ROW_TILE, t)
    nt = t // tq
    return pl.pallas_call(
        _mem_kernel,
        out_shape=jax.ShapeDtypeStruct((batch * t, MEM_WIDTH), out_dtype),
        grid=(batch, nt),
        in_specs=[pl.BlockSpec((tq, MEM_WIDTH), lambda b, i: (b * nt + i, qcol)),
                  pl.BlockSpec((1, N_MEM, MEM_WIDTH), lambda b, i: (b + kboff, 0, kcol)),
                  pl.BlockSpec((1, N_MEM, MEM_WIDTH), lambda b, i: (b + kboff, 0, vcol))],
        out_specs=pl.BlockSpec((tq, MEM_WIDTH), lambda b, i: (b * nt + i, 0)),
        compiler_params=_cparams(("parallel", "parallel")),
        name=name,
    )(q, k_arr, v_arr)


def _gate_norm_store(y_ref, ygrp, z_ref, gn_ref, g):
    sl = slice(g * GROUP_W, (g + 1) * GROUP_W)
    ygrp = ygrp * _silu(z_ref[:, sl].astype(F32))
    y_ref[:, sl] = _rms(ygrp, gn_ref[:, sl]).astype(y_ref.dtype)


def _ssd_prompt_kernel(xs_ref, bc_ref, z_ref, dt_ref, cw_ref, cb_ref, dtb_ref, alog_ref, d_ref, gn_ref, e_ref, sh_ref,
                       y_ref, hout_ref, extx, extbc, ht):
    q = SSD_CHUNK
    c = pl.program_id(1)
    tail = 16

    @pl.when(c == 0)
    def _():
        extx[0:q] = jnp.zeros((q, D_INNER), BF16)
        extbc[0:q] = jnp.zeros((q, CONV_DIM - D_INNER), BF16)
        ht[...] = jnp.zeros(ht.shape, F32)

    @pl.when(c > 0)
    def _():
        extx[q - tail:q] = extx[2 * q - tail:2 * q]
        extbc[q - tail:q] = extbc[2 * q - tail:2 * q]

    extx[q:2 * q] = xs_ref[...]
    extbc[q:2 * q] = bc_ref[...]

    def conv(ext, w, b):
        sh = _dot(sh_ref[...], ext[...])
        acc = (b + w[3:4] * ext[q:2 * q].astype(F32) + w[2:3] * sh[0:q] + w[1:2] * sh[q:2 * q]
               + w[0:1] * sh[2 * q:3 * q])
        return _silu(acc)

    cw = cw_ref[...]
    cb = cb_ref[...]
    xs = conv(extx, cw[:, :D_INNER], cb[:, :D_INNER])
    bcv = conv(extbc, cw[:, D_INNER:], cb[:, D_INNER:])

    dt = _softplus(dt_ref[...] + dtb_ref[...])
    a = dt * (-jnp.exp(alog_ref[...]))
    row = lax.broadcasted_iota(jnp.int32, (q, q), 0)
    col = lax.broadcasted_iota(jnp.int32, (q, q), 1)
    causal = row >= col
    acs = _sel_dot(jnp.where(causal, 1.0, 0.0).astype(BF16), a, terms=3)
    acs_t = acs.T
    a_last = acs[q - 1:q, :]
    dtx = _expand(dt, e_ref)
    dwx = _expand(dt * jnp.exp(a_last - acs), e_ref)
    eax = _expand(jnp.exp(acs), e_ref)
    xdt = xs * dtx
    xw_b = (xs * dwx).astype(BF16)
    lane = lax.broadcasted_iota(jnp.int32, (q, LANES), 1)
    heads_per_group = SSD_HEADS // SSD_GROUPS

    for g in range(SSD_GROUPS):
        gs = slice(g * GROUP_W, (g + 1) * GROUP_W)
        bg = bcv[:, g * D_STATE:(g + 1) * D_STATE].astype(BF16)
        cg = bcv[:, (SSD_GROUPS + g) * D_STATE:(SSD_GROUPS + g + 1) * D_STATE].astype(BF16)
        cbm = _dot_nt(cg, bg)
        htg = ht[:, gs]
        y_off = _dot(cg, htg.astype(BF16)) * eax[:, gs]
        ht[:, gs] = eax[q - 1:q, gs] * htg + _dot_tn(bg, xw_b[:, gs])
        pairs = []
        for j in range(heads_per_group // 2):
            h0 = g * heads_per_group + 2 * j
            xpair = xdt[:, h0 * SSD_HEAD_DIM:(h0 + 2) * SSD_HEAD_DIM]
            yp = None
            for k in range(2):
                h = h0 + k
                seg = acs[:, h:h + 1] - acs_t[h:h + 1, :]
                m = (cbm * jnp.exp(jnp.where(causal, seg, NEG))).astype(BF16)
                keep = (lane < SSD_HEAD_DIM) if k == 0 else (lane >= SSD_HEAD_DIM)
                t = _dot(m, jnp.where(keep, xpair, 0.0).astype(BF16))
                yp = t if yp is None else yp + t
            pairs.append(yp)
        ygrp = jnp.concatenate(pairs, axis=1) + y_off + d_ref[:, gs] * xs[:, gs]
        _gate_norm_store(y_ref, ygrp, z_ref, gn_ref, g)

    @pl.when(c == pl.num_programs(1) - 1)
    def _():
        hout_ref[0] = ht[...].T


def _ssd_prompt(proj, dt, sp, batch, t):
    nc = t // SSD_CHUNK
    q = SSD_CHUNK
    const = lambda b, c: (0, 0)
    return pl.pallas_call(
        _ssd_prompt_kernel,
        out_shape=(jax.ShapeDtypeStruct((batch * t, D_INNER), BF16),
                   jax.ShapeDtypeStruct((batch, D_INNER, D_STATE), F32)),
        grid=(batch, nc),
        in_specs=[pl.BlockSpec((q, D_INNER), lambda b, c: (b * nc + c, 0)),
                  pl.BlockSpec((q, 1024), lambda b, c: (b * nc + c, 2)),
                  pl.BlockSpec((q, D_INNER), lambda b, c: (b * nc + c, 2)),
                  pl.BlockSpec((q, LANES), lambda b, c: (b * nc + c, 0)),
                  pl.BlockSpec((4, CONV_DIM), const), pl.BlockSpec((1, CONV_DIM), const),
                  pl.BlockSpec((1, LANES), const), pl.BlockSpec((1, LANES), const),
                  pl.BlockSpec((1, D_INNER), const), pl.BlockSpec((1, D_INNER), const),
                  pl.BlockSpec((LANES, D_INNER), const),
                  pl.BlockSpec((3 * q, 2 * q), const)],
        out_specs=(pl.BlockSpec((q, D_INNER), lambda b, c: (b * nc + c, 0)),
                   pl.BlockSpec((1, D_INNER, D_STATE), lambda b, c: (b, 0, 0))),
        scratch_shapes=[pltpu.VMEM((2 * q, D_INNER), BF16),
                        pltpu.VMEM((2 * q, CONV_DIM - D_INNER), BF16),
                        pltpu.VMEM((D_STATE, D_INNER), F32)],
        compiler_params=_cparams(("parallel", "arbitrary"), vmem_mb=48),
        name="ssd_prompt",
    )(proj, proj, proj, dt, sp["conv_w"], sp["conv_b"], sp["dt_bias"], sp["a_log"], sp["d_x"], sp["g_norm"], sp["expand"],
      _conv_shift_matrix(q))


def _conv_shift_matrix(q):
    row = jnp.arange(3 * q, dtype=jnp.int32)[:, None]
    col = jnp.arange(2 * q, dtype=jnp.int32)[None, :]
    return (col == q + row % q - (row // q + 1)).astype(BF16)


def _ssd_sample_kernel(xs_ref, bc_ref, z_ref, dt_ref, cs_ref, h0_ref, cw_ref, cb_ref, dtb_ref, alog_ref, d_ref,
                       gn_ref, e_ref, *rest, n_carried, out_layer):
    y_ref, cnew_ref, hnew_ref, ext = rest[n_carried:]
    t = xs_ref.shape[0]
    for other in range(cnew_ref.shape[0]):
        if other != out_layer:
            cnew_ref[other] = jnp.zeros(cnew_ref.shape[1:], F32)
            hnew_ref[other] = jnp.zeros(hnew_ref.shape[1:], F32)
    ext[5:8] = cs_ref[0]
    ext[8:8 + t, 0:D_INNER] = xs_ref[...]
    ext[8:8 + t, D_INNER:CONV_DIM] = bc_ref[...]
    cw = cw_ref[...]
    acc = (cb_ref[...] + cw[0:1] * ext[5:5 + t] + cw[1:2] * ext[6:6 + t]
           + cw[2:3] * ext[7:7 + t] + cw[3:4] * ext[8:8 + t])
    cnew_ref[out_layer, 0] = ext[5 + t:8 + t]
    xbc = _silu(acc)
    xs = xbc[:, :D_INNER]

    dt = _softplus(dt_ref[...] + dtb_ref[...])
    a = dt * (-jnp.exp(alog_ref[...]))
    rowi = lax.broadcasted_iota(jnp.int32, (t, LANES), 0)
    acs = jnp.zeros((t, LANES), F32)
    for s in range(t):
        acs = acs + jnp.where(rowi >= s, a[s:s + 1, :], 0.0)
    a_last = acs[t - 1:t, :]
    eacs = jnp.exp(acs)
    acsx = _expand(acs, e_ref, terms=3)
    dtx = _expand(dt, e_ref)
    dwx = _expand(dt * jnp.exp(a_last - acs), e_ref)
    eax = _expand(eacs, e_ref)
    xdt = xs * dtx
    xw_b = (xs * dwx).astype(BF16)

    def grp(base, g):
        return xbc[:, D_INNER + (base + g) * D_STATE:D_INNER + (base + g + 1) * D_STATE]

    li = lax.broadcasted_iota(jnp.int32, (t, D_INNER), 0)
    y = jnp.zeros((t, D_INNER), F32)
    for s in range(t):
        decay = jnp.exp(jnp.where(li >= s, acsx - acsx[s:s + 1, :], NEG))
        cbx = jnp.concatenate(
            [jnp.broadcast_to(jnp.sum(grp(SSD_GROUPS, g) * grp(0, g)[s:s + 1, :], axis=-1, keepdims=True),
                              (t, GROUP_W)) for g in range(SSD_GROUPS)], axis=1)
        y = y + cbx * decay * xdt[s:s + 1, :]

    last_only = jnp.where(rowi == t - 1, eacs, 0.0)
    ones = jnp.ones((t, LANES), BF16)
    rdec = sum(_dot_tn(p, ones) for p in _split_bf16(last_only, 3))

    heads_per_group = SSD_HEADS // SSD_GROUPS
    for g in range(SSD_GROUPS):
        gs = slice(g * GROUP_W, (g + 1) * GROUP_W)
        bg = grp(0, g).astype(BF16)
        cg = grp(SSD_GROUPS, g).astype(BF16)
        h0g = h0_ref[0, gs, :]
        y_off = _dot_nt(cg, h0g.astype(BF16)) * eax[:, gs]
        upd = _dot_tn(xw_b[:, gs], bg)
        for hh in range(heads_per_group):
            h = g * heads_per_group + hh
            rs = slice(hh * SSD_HEAD_DIM, (hh + 1) * SSD_HEAD_DIM)
            hnew_ref[out_layer, 0, h * SSD_HEAD_DIM:(h + 1) * SSD_HEAD_DIM, :] = (
                rdec[h:h + 1, :] * h0g[rs, :] + upd[rs, :])
        ygrp = y[:, gs] + y_off + d_ref[:, gs] * xs[:, gs]
        _gate_norm_store(y_ref, ygrp, z_ref, gn_ref, g)


def _ssd_sample(proj, dt, conv_all, ssm_all, stacked, layer, sp, batch, t):
    const = lambda b: (0, 0)
    n_layers = ssm_all.shape[0]
    conv_spec = pl.BlockSpec((None, 1, 3, CONV_DIM), lambda b: (layer, b, 0, 0))
    ssm_spec = pl.BlockSpec((None, 1, D_INNER, D_STATE), lambda b: (layer, b, 0, 0))
    in_specs = [pl.BlockSpec((t, D_INNER), lambda b: (b, 0)),
                pl.BlockSpec((t, 1024), lambda b: (b, 2)),
                pl.BlockSpec((t, D_INNER), lambda b: (b, 2)),
                pl.BlockSpec((t, LANES), lambda b: (b, 0)),
                conv_spec, ssm_spec,
                pl.BlockSpec((4, CONV_DIM), const), pl.BlockSpec((1, CONV_DIM), const),
                pl.BlockSpec((1, LANES), const), pl.BlockSpec((1, LANES), const),
                pl.BlockSpec((1, D_INNER), const), pl.BlockSpec((1, D_INNER), const),
                pl.BlockSpec((LANES, D_INNER), const)]
    args = [proj, proj, proj, dt, conv_all, ssm_all, sp["conv_w"], sp["conv_b"], sp["dt_bias"], sp["a_log"],
            sp["d_x"], sp["g_norm"], sp["expand"]]
    if stacked is None:
        aliases = {}
        out_layer = layer
        conv_out = pl.BlockSpec((n_layers, 1, 3, CONV_DIM), lambda b: (0, b, 0, 0))
        ssm_out = pl.BlockSpec((n_layers, 1, D_INNER, D_STATE), lambda b: (0, b, 0, 0))
    else:
        aliases = {len(args): 1, len(args) + 1: 2}
        in_specs += [pl.BlockSpec(memory_space=pl.ANY), pl.BlockSpec(memory_space=pl.ANY)]
        args += list(stacked)
        out_layer = 0
        conv_out = pl.BlockSpec((1, 1, 3, CONV_DIM), lambda b: (layer, b, 0, 0))
        ssm_out = pl.BlockSpec((1, 1, D_INNER, D_STATE), lambda b: (layer, b, 0, 0))
    y, conv_new, ssm_new = pl.pallas_call(
        functools.partial(_ssd_sample_kernel, n_carried=len(aliases), out_layer=out_layer),
        out_shape=(jax.ShapeDtypeStruct((batch * t, D_INNER), F32),
                   jax.ShapeDtypeStruct((n_layers, batch, 3, CONV_DIM), F32),
                   jax.ShapeDtypeStruct((n_layers, batch, D_INNER, D_STATE), F32)),
        grid=(batch,),
        in_specs=in_specs,
        out_specs=(pl.BlockSpec((t, D_INNER), lambda b: (b, 0)), conv_out, ssm_out),
        scratch_shapes=[pltpu.VMEM((8 + t, CONV_DIM), F32)],
        input_output_aliases=aliases,
        compiler_params=_cparams(("parallel",)),
        name="ssd_sample",
    )(*args)
    return y, (conv_new, ssm_new)


def _flash_kernel(qt_ref, kt_ref, q_ref, k_ref, v_ref, o_ref, m_ref, acc_ref, *, sub):
    step = pl.program_id(2)
    qi = qt_ref[step]
    ki = kt_ref[step]
    n_heads, tq, _ = q_ref.shape
    nsub = tq // sub

    @pl.when(ki == 0)
    def _():
        m_ref[...] = jnp.full(m_ref.shape, NEG, F32)
        acc_ref[...] = jnp.zeros(acc_ref.shape, F32)

    def unit(pair, qs, ks, masked, k_width=1):
        rows = slice(qs * sub, (qs + 1) * sub)
        cols = slice(ks * sub, (ks + k_width) * sub)
        for h in range(2):
            hd = pair * 2 + h
            s = _dot(q_ref[hd, rows, :], k_ref[pl.ds(pl.multiple_of(hd * LANES, LANES), LANES), cols])
            if masked:
                r = lax.broadcasted_iota(jnp.int32, (sub, sub), 0)
                c = lax.broadcasted_iota(jnp.int32, (sub, sub), 1)
                s = jnp.where(r >= c, s, NEG)
            m_prev = m_ref[hd, rows, :]
            m_new = jnp.maximum(m_prev, jnp.max(s, axis=-1, keepdims=True))
            p = jnp.exp2(s - jnp.concatenate([m_new] * (s.shape[1] // LANES), axis=1))
            acc_ref[hd, rows, :] = (acc_ref[hd, rows, :] * jnp.exp2(m_prev - m_new)
                                    + _dot(p.astype(BF16), v_ref[hd, cols, :]))
            m_ref[hd, rows, :] = m_new

    def head_pairs(diagonal):
        def body(pair, carry):
            if diagonal:
                for ks in range(nsub):
                    for qs in range(ks, nsub):
                        unit(pair, qs, ks, ks == qs)
            else:
                for qs in range(nsub):
                    unit(pair, qs, 0, False, k_width=nsub)
            return carry
        lax.fori_loop(0, n_heads // 2, body, 0)

    @pl.when(ki < qi)
    def _():
        head_pairs(False)

    @pl.when(ki == qi)
    def _():
        head_pairs(True)
        for pair in range(n_heads // 2):
            outs = []
            for hd in (2 * pair, 2 * pair + 1):
                a = acc_ref[hd]
                outs.append((a * (1.0 / a[:, V_HEAD:V_HEAD + 1]))[:, :V_HEAD])
            o_ref[:, pair * LANES:(pair + 1) * LANES] = jnp.concatenate(outs, axis=1).astype(o_ref.dtype)


def _flash_attend(q2, k_all, v_all, batch, t):
    tq = min(ATT_TILE, t)
    nq = t // tq
    pairs = [(qi, ki) for qi in range(nq) for ki in range(qi + 1)]
    qt = jnp.asarray([p[0] for p in pairs], jnp.int32)
    kt = jnp.asarray([p[1] for p in pairs], jnp.int32)
    hps = FLASH_HEADS_PER_STEP
    grid_spec = pltpu.PrefetchScalarGridSpec(
        num_scalar_prefetch=2,
        grid=(batch, MLA_HEADS // hps, len(pairs)),
        in_specs=[pl.BlockSpec((hps, tq, LANES), lambda b, hg, s, qt, kt: (hg, b * nq + qt[s], 0)),
                  pl.BlockSpec((hps * LANES, tq), lambda b, hg, s, qt, kt: (hg, b * nq + kt[s])),
                  pl.BlockSpec((hps, tq, LANES), lambda b, hg, s, qt, kt: (hg, b * nq + kt[s], 0))],
        out_specs=pl.BlockSpec((tq, hps * V_HEAD), lambda b, hg, s, qt, kt: (b * nq + qt[s], hg)),
        scratch_shapes=[pltpu.VMEM((hps, tq, LANES), F32), pltpu.VMEM((hps, tq, LANES), F32)],
    )
    return pl.pallas_call(
        functools.partial(_flash_kernel, sub=min(ATT_SUB, tq)),
        out_shape=jax.ShapeDtypeStruct((batch * t, MLA_HEADS * V_HEAD), BF16),
        grid_spec=grid_spec,
        compiler_params=_cparams(("parallel", "parallel", "arbitrary")),
        name="mla_flash",
    )(qt, kt, q2, k_all, v_all)


def _decode_kernel(pt_ref, q_ref, cn_ref, pn_ref, ckv_hbm, kpe_hbm, o_ref, ckv_buf, kpe_buf, sem, kv_ref, s_ref,
                   *, n_pages, page, chunk):
    b = pl.program_id(0)
    last = pl.num_programs(0) - 1
    slot = b % 2
    rows = q_ref.shape[1]
    n_chunks = n_pages * page // chunk
    pages_per_chunk = n_pages // n_chunks

    def page_copies(pid, sl, p):
        rows_p = pl.ds(p * page, page)
        return (pltpu.make_async_copy(ckv_hbm.at[pid], ckv_buf.at[sl, rows_p, :], sem.at[0, sl]),
                pltpu.make_async_copy(kpe_hbm.at[pid], kpe_buf.at[sl, :, rows_p], sem.at[1, sl]))

    def start_pages(bb, sl, p0, p1):
        for p in range(p0, p1):
            for cp in page_copies(pt_ref[bb * n_pages + p], sl, p):
                cp.start()

    def wait_pages(sl):
        for p in range(n_pages):
            for cp in page_copies(0, sl, p):
                cp.wait()

    @pl.when(b == 0)
    def _():
        start_pages(0, 0, 0, n_pages)

    wait_pages(slot)
    nxt = jnp.minimum(b + 1, last)

    q = q_ref[0]
    ql = q[:, :KV_LORA]
    qp = q[:, KV_LORA:KV_LORA + QK_ROPE]
    past = n_pages * page

    for c in range(n_chunks):
        start_pages(nxt, 1 - slot, c * pages_per_chunk, (c + 1) * pages_per_chunk)
        cs = slice(c * chunk, (c + 1) * chunk)
        kc = ckv_buf[slot, cs, :].astype(BF16)
        kp = kpe_buf[slot, :, cs].astype(BF16)
        kv_ref[cs, :] = kc
        s_ref[:, cs] = _dot_nt(ql, kc) + _dot(qp, kp)

    t_new = cn_ref.shape[1]
    pad = jnp.zeros((LANES - t_new, KV_LORA), F32)
    kc = jnp.concatenate([cn_ref[0], pad], axis=0).astype(BF16)
    kp = jnp.concatenate([pn_ref[0], pad[:, :LANES]], axis=0).astype(BF16)
    s = _dot_nt(ql, kc) + _dot_nt(q[:, KV_LORA:], kp)
    r = lax.broadcasted_iota(jnp.int32, (rows, LANES), 0)
    c = lax.broadcasted_iota(jnp.int32, (rows, LANES), 1)
    kv_ref[past:, :] = kc
    s_ref[:, past:] = jnp.where(c <= r // MLA_HEADS, s, NEG)

    s_all = s_ref[...]
    p = jnp.exp2(s_all - jnp.max(s_all, axis=-1, keepdims=True))
    l = jnp.sum(p, axis=-1, keepdims=True)
    o_ref[0] = (_dot(p.astype(BF16), kv_ref[...]) / l).astype(o_ref.dtype)

    @pl.when(b == last)
    def _():
        wait_pages(1 - slot)


def _decode_attend(q_abs, ckv_new, kpe_new, cache_ckv, cache_kpe_t, page_table):
    batch, rows, _ = q_abs.shape
    n_pages = page_table.shape[1]
    page = cache_ckv.shape[1]
    past = n_pages * page
    t_new = ckv_new.shape[1]
    per_b = lambda b, pt: (b, 0, 0)
    grid_spec = pltpu.PrefetchScalarGridSpec(
        num_scalar_prefetch=1,
        grid=(batch,),
        in_specs=[pl.BlockSpec((1, rows, LAT_W), per_b),
                  pl.BlockSpec((1, t_new, KV_LORA), per_b),
                  pl.BlockSpec((1, t_new, LANES), per_b),
                  pl.BlockSpec(memory_space=pl.ANY),
                  pl.BlockSpec(memory_space=pl.ANY)],
        out_specs=pl.BlockSpec((1, rows, KV_LORA), per_b),
        scratch_shapes=[pltpu.VMEM((2, past, KV_LORA), F32),
                        pltpu.VMEM((2, QK_ROPE, past), F32),
                        pltpu.SemaphoreType.DMA((2, 2)),
                        pltpu.VMEM((past + LANES, KV_LORA), BF16),
                        pltpu.VMEM((rows, past + LANES), F32)],
    )
    return pl.pallas_call(
        functools.partial(_decode_kernel, n_pages=n_pages, page=page, chunk=min(DECODE_CHUNK, past)),
        out_shape=jax.ShapeDtypeStruct((batch, rows, KV_LORA), BF16),
        grid_spec=grid_spec,
        compiler_params=_cparams(("arbitrary",)),
        name="mla_decode",
    )(page_table.reshape(-1), q_abs, ckv_new, kpe_new, cache_ckv, cache_kpe_t)


def _pad_lanes(v, width):
    return jnp.pad(v, [(0, 0)] * (v.ndim - 1) + [(0, width - v.shape[-1])])


def _ssd_params(i, w_in_a, conv_w, conv_b, dt_bias, a_log, d_skip, g_ssd_norm):
    w = w_in_a[i]
    z0, x0, d0, m0 = 0, D_INNER, D_INNER + CONV_DIM, D_INNER + CONV_DIM + SSD_HEADS
    w_main = jnp.concatenate([w[:, x0:d0], w[:, m0:], w[:, z0:x0]], axis=1).astype(BF16)
    w_dt = _pad_lanes(w[:, d0:m0], LANES).astype(BF16)
    head_of = jnp.arange(D_INNER, dtype=jnp.int32) // SSD_HEAD_DIM
    expand = (jnp.arange(LANES, dtype=jnp.int32)[:, None] == head_of[None, :]).astype(BF16)
    return dict(w_main=w_main, w_dt=w_dt, conv_w=conv_w[i], conv_b=conv_b[i].reshape(1, CONV_DIM),
                dt_bias=_pad_lanes(dt_bias[i].reshape(1, SSD_HEADS), LANES),
                a_log=_pad_lanes(a_log[i].reshape(1, SSD_HEADS), LANES),
                d_x=jnp.repeat(d_skip[i], SSD_HEAD_DIM).reshape(1, D_INNER),
                g_norm=g_ssd_norm[i].reshape(1, D_INNER), expand=expand)


def _mla_shared_params(w_kv_a, w_uk, w_uv):
    half = QK_ROPE // 2
    wa = _pad_lanes(w_kv_a, LAT_W).astype(BF16)
    pe = w_kv_a[:, KV_LORA:]
    wr = _pad_lanes(jnp.concatenate([-pe[:, half:], pe[:, :half]], axis=1), LANES).astype(BF16)
    eye = jnp.eye(QK_ROPE, dtype=F32)
    k_blocks, abs_blocks = [], []
    for h in range(MLA_HEADS):
        kb = jnp.zeros((LAT_W, LANES), F32)
        kb = kb.at[:KV_LORA, :QK_NOPE].set(w_uk[:, h, :])
        kb = kb.at[KV_LORA:KV_LORA + QK_ROPE, QK_NOPE:QK_NOPE + QK_ROPE].set(eye)
        k_blocks.append(kb)
        ab = jnp.zeros((LANES, LAT_W), F32)
        ab = ab.at[:QK_NOPE, :KV_LORA].set(w_uk[:, h, :].T)
        ab = ab.at[QK_NOPE:QK_NOPE + QK_ROPE, KV_LORA:KV_LORA + QK_ROPE].set(eye)
        abs_blocks.append(ab)
    w_k = jnp.concatenate(k_blocks, axis=1).T.astype(BF16)
    v_blocks = []
    for h in range(MLA_HEADS):
        vb = jnp.zeros((LAT_W, LANES), F32).at[:KV_LORA, :V_HEAD].set(w_uv[:, h, :])
        v_blocks.append(vb.at[KV_LORA + QK_ROPE, V_HEAD].set(1.0))
    w_v = jnp.concatenate(v_blocks, axis=1).astype(BF16)
    w_abs = jnp.stack(abs_blocks).astype(BF16)
    zero = jnp.zeros((KV_LORA, V_HEAD), F32)
    w_uvp = jnp.stack([jnp.block([[w_uv[:, 2 * j, :], zero], [zero, w_uv[:, 2 * j + 1, :]]])
                       for j in range(MLA_HEADS // 2)]).astype(BF16)
    return dict(wa=wa, wr=wr, w_k=w_k, w_v=w_v, w_abs=w_abs, w_uvp=w_uvp)


def _mla_layer_params(i, w_in_b, w_q_b):
    half = QK_ROPE // 2
    w = w_in_b[i]
    w_inb = jnp.concatenate([w[:, Q_LORA:], w[:, :Q_LORA]], axis=1).astype(BF16)
    wq = w_q_b[i]
    wqa = _pad_lanes(wq, LANES).reshape(Q_LORA, MLA_HEADS * LANES).astype(BF16)
    rot = jnp.concatenate([jnp.zeros_like(wq[..., :QK_NOPE]), -wq[..., QK_NOPE + half:], wq[..., QK_NOPE:QK_NOPE + half]],
                          axis=-1)
    wqb = _pad_lanes(rot, LANES).reshape(Q_LORA, MLA_HEADS * LANES).astype(BF16)
    return dict(w_inb=w_inb, wqa=wqa, wqb=wqb)


def _rope_tables(pos):
    half = QK_ROPE // 2
    inv = ROPE_BASE ** (-jnp.arange(half, dtype=F32) / half)
    ang = pos.astype(F32)[:, None] * inv[None, :]
    cos, sin = jnp.cos(ang), jnp.sin(ang)
    n = pos.shape[0]
    one = lambda w: jnp.ones((n, w), F32)
    zero = lambda w: jnp.zeros((n, w), F32)
    tail = LANES - QK_NOPE - QK_ROPE
    cos_q = jnp.concatenate([one(QK_NOPE), cos, cos, one(tail)], axis=1)
    sin_q = jnp.concatenate([zero(QK_NOPE), sin, sin, zero(tail)], axis=1)
    cos_k = jnp.concatenate([cos, cos, one(LANES - QK_ROPE)], axis=1)
    sin_k = jnp.concatenate([sin, sin, zero(LANES - QK_ROPE)], axis=1)
    return cos_q, sin_q, cos_k, sin_k


def _trunk(x, batch, t, sample, mem_attend, ssd_state, mla_cache, tabs, P):
    act = F32 if sample else BF16
    cos_q, sin_q, cos_k, sin_k = tabs
    ssm_out, conv_out, stacked = [], [], None
    ckv = kpe = None
    for l in range(DEPTH):
        if l < N_A:
            sp = P["ssd"][l]
            proj = _proj(x, sp["w_main"], g=P["g_pre_mix"][l], out_dtype=act, tn=1024, row_tiles=2, name=f"in_a{l}")
            dt = _proj(x, sp["w_dt"], g=P["g_pre_mix"][l], out_dtype=F32, name=f"in_dt{l}")
            if sample:
                state_ssm, state_conv = ssd_state
                y, stacked = _ssd_sample(proj, dt, state_conv, state_ssm.reshape(N_A, batch, D_INNER, D_STATE),
                                         stacked, l, sp, batch, t)
            else:
                y, h_new = _ssd_prompt(proj, dt, sp, batch, t)
                ssm_out.append(h_new)
                conv_out.append(proj.reshape(batch, t, -1)[:, t - 3:, :CONV_DIM].astype(F32))
            mo = mem_attend(l, proj, 3)
            w_out = P["w_out_a"][l]
            x = _outproj(y, mo, w_out[:D_INNER], w_out[D_INNER:], P["g_post_mix"][l], x, f"out_a{l}")
        else:
            i = l - N_A
            ms, ml = P["mla_shared"], P["mla"][i]
            if i == 0:
                ckv, kpe, lat = _kv_latent(x, P["g_kv_in"], ms["wa"], ms["wr"], P["g_kv_norm"], cos_k, sin_k)
                if not sample:
                    k_all = _proj_t(lat, ms["w_k"], "k_heads")
                    v_all = _proj(lat, ms["w_v"], tn=1024, head_major=True, name="v_heads")
            proj = _proj(x, ml["w_inb"], g=P["g_pre_mix"][l], out_dtype=act, name=f"in_b{i}")
            q2 = _proj(proj, ml["wqa"], g=P["g_q_norm"][i], xcol=2, tn=1024,
                       rope=(ml["wqb"], cos_q, sin_q, MLA_SCALE * LOG2E), head_major=not sample,
                       name=f"q_heads{i}")
            if sample:
                cache_ckv, cache_kpe, page_table = mla_cache
                q_abs = _block_proj(q2, ms["w_abs"], f"q_abs{i}").reshape(batch, t * MLA_HEADS, LAT_W)
                o_lat = _decode_attend(q_abs, ckv.reshape(batch, t, KV_LORA), kpe.reshape(batch, t, LANES),
                                       cache_ckv, cache_kpe, page_table)
                o = _block_proj(o_lat.reshape(batch * t, MLA_HEADS * KV_LORA), ms["w_uvp"], f"o_heads{i}")
            else:
                o = _flash_attend(q2, k_all, v_all, batch, t)
            mo = mem_attend(l, proj, 0)
            w_out = P["w_out_b"][i]
            n_o = MLA_HEADS * V_HEAD
            x = _outproj(o, mo, w_out[:n_o], w_out[n_o:], P["g_post_mix"][l], x, f"out_b{i}")
        x = _ffn(x, P["g_pre_ffn"][l], P["w_ffn_up"][l], P["w_ffn_down"][l], P["g_post_ffn"][l], f"ffn{l}")
    conv_new, ssm_new = stacked if sample else (jnp.stack(conv_out), jnp.stack(ssm_out))
    ssm_new = ssm_new.reshape(N_A, batch, SSD_HEADS, SSD_HEAD_DIM, D_STATE)
    return x, ssm_new, conv_new, ckv, kpe[:, :QK_ROPE]


def kernel(x_prompt, x_sample, mem_prompt, state_ssm, state_conv, cache_ckv, cache_kpe, cache_mem_k, cache_mem_v, page_table, g_pre_mix, g_post_mix, g_pre_ffn, g_post_ffn, w_ffn_up, w_ffn_down, w_mem_k, w_mem_v, w_in_a, conv_w, conv_b, dt_bias, a_log, d_skip, g_ssd_norm, w_out_a, g_kv_in, w_kv_a, g_kv_norm, w_uk, w_uv, w_in_b, g_q_norm, w_q_b, w_out_b):
    b, s, _ = x_prompt.shape
    db, t, _ = x_sample.shape
    past_len = page_table.shape[1] * cache_ckv.shape[1]

    P = dict(
        g_pre_mix=g_pre_mix, g_post_mix=g_post_mix, g_pre_ffn=g_pre_ffn, g_post_ffn=g_post_ffn,
        g_kv_in=g_kv_in, g_kv_norm=g_kv_norm, g_q_norm=g_q_norm,
        w_ffn_up=w_ffn_up.astype(BF16), w_ffn_down=w_ffn_down.astype(BF16),
        w_out_a=w_out_a.astype(BF16), w_out_b=w_out_b.astype(BF16),
        ssd=[_ssd_params(i, w_in_a, conv_w, conv_b, dt_bias, a_log, d_skip, g_ssd_norm) for i in range(N_A)],
        mla_shared=_mla_shared_params(w_kv_a, w_uk, w_uv),
        mla=[_mla_layer_params(i, w_in_b, w_q_b) for i in range(DEPTH - N_A)],
    )

    w_mem = jnp.concatenate([w_mem_k[l] for l in range(DEPTH)] + [w_mem_v[l] for l in range(DEPTH)], axis=1)
    mem_kv = _proj(mem_prompt.reshape(b * N_MEM, D_MODEL), w_mem.astype(BF16), out_dtype=F32, tn=1024, name="mem_kv")
    mem_kv5 = mem_kv.reshape(b, N_MEM, 2 * DEPTH, MEM_HEADS, MEM_HEAD_DIM)
    p_mem_k = jnp.transpose(mem_kv5[:, :, :DEPTH], (2, 0, 1, 3, 4))
    p_mem_v = jnp.transpose(mem_kv5[:, :, DEPTH:], (2, 0, 1, 3, 4))
    mem_kv3 = mem_kv.reshape(b, N_MEM, 2 * DEPTH * MEM_WIDTH)
    tabs_p = _rope_tables(jnp.arange(s, dtype=jnp.int32))
    y_prompt, p_ssm, p_conv, p_ckv, p_kpe = _trunk(
        x_prompt.reshape(b * s, D_MODEL), b, s, False,
        lambda l, q, qcol: _mem_attend(q, qcol, mem_kv3, mem_kv3, 0, l, DEPTH + l, b, s, BF16, f"mem{l}"),
        None, None, tabs_p, P)

    tabs_s = tuple(jnp.tile(tb, (db, 1)) for tb in _rope_tables(past_len + jnp.arange(t, dtype=jnp.int32)))
    cache_kpe_t = jnp.swapaxes(cache_kpe, 1, 2)
    y_sample, s_ssm, s_conv, s_ckv, s_kpe = _trunk(
        x_sample.reshape(db * t, D_MODEL), db, t, True,
        lambda l, q, qcol: _mem_attend_cache(q, qcol, cache_mem_k, cache_mem_v, l, db, t, f"mem_s{l}"),
        (state_ssm, state_conv), (cache_ckv, cache_kpe_t, page_table), tabs_s, P)

    return (y_prompt.reshape(b, s, D_MODEL), y_sample.reshape(db, t, D_MODEL),
            p_ssm, p_conv, p_ckv.reshape(b, s, KV_LORA), p_kpe.reshape(b, s, QK_ROPE), p_mem_k, p_mem_v,
            s_ssm, s_conv, s_ckv.reshape(db, t, KV_LORA), s_kpe.reshape(db, t, QK_ROPE))
```
